```python
import math
import jax
import jax.numpy as jnp
from jax import lax
import numpy as np

D_MODEL = 2048
BATCH = 1
SEQ = 8192
DEPTH = 2

CHUNK = 64
Q_BLOCK = 128
EPS = 1e-6
PLE_DIM = 256
D_FF = 5504
SSD_WIDTH = D_MODEL
SSD_HEAD_DIM = 64
SSD_HEADS = SSD_WIDTH // SSD_HEAD_DIM
SSD_GROUPS = 4
SSD_HPG = SSD_HEADS // SSD_GROUPS
SSD_STATE = 128
SSD_CONV = 4
SSD_CONV_DIM = SSD_WIDTH + 2 * SSD_GROUPS * SSD_STATE
HGRN_WIDTH = D_MODEL
HGRN_KDIM = 128
HGRN_HEADS = HGRN_WIDTH // HGRN_KDIM
HGRN_VDIM = HGRN_WIDTH // HGRN_HEADS
FOX_WIDTH = D_MODEL
FOX_HEAD_DIM = 128
FOX_HEADS = FOX_WIDTH // FOX_HEAD_DIM

N_EVEN = (DEPTH + 1) // 2
N_ODD = DEPTH // 2
AB_SPLITS = (
    SSD_WIDTH,
    SSD_WIDTH + SSD_CONV_DIM,
    SSD_WIDTH + SSD_CONV_DIM + SSD_HEADS,
    SSD_WIDTH + SSD_CONV_DIM + SSD_HEADS + HGRN_WIDTH,
    SSD_WIDTH + SSD_CONV_DIM + SSD_HEADS + 2 * HGRN_WIDTH,
    SSD_WIDTH + SSD_CONV_DIM + SSD_HEADS + 3 * HGRN_WIDTH,
)
AB_IN = SSD_WIDTH + SSD_CONV_DIM + SSD_HEADS + 4 * HGRN_WIDTH
AB_OUT = SSD_WIDTH + HGRN_WIDTH
FOX_IN = 3 * FOX_WIDTH + FOX_HEADS

kernel_name = "hybrid_ssd_hgrn2_fox_macaron_trunk"


def rmsnorm(x, w):
    xf = x.astype(jnp.float32)
    y = xf * lax.rsqrt(jnp.mean(xf * xf, axis=-1, keepdims=True) + EPS)
    return (y * w.astype(jnp.float32)).astype(x.dtype)


def swiglu_half(h, norm_w, w_in, w_out):
    gate, up = jnp.split(rmsnorm(h, norm_w) @ w_in, 2, axis=-1)
    return h + 0.5 * ((jax.nn.silu(gate) * up) @ w_out)


def causal_dwconv(x, w, b):
    k = w.shape[0]
    y = lax.conv_general_dilated(
        x, w[:, None, :].astype(x.dtype), window_strides=(1,), padding=[(k - 1, 0)],
        dimension_numbers=("NWC", "WIO", "NWC"), feature_group_count=x.shape[-1])
    return y + b.astype(x.dtype)


def chunk_state_scan(states, decay):
    def step(carry, inp):
        s_c, a_c = inp
        return carry * a_c + s_c, carry
    init = jnp.zeros_like(states[:, 0])
    _, prev = lax.scan(step, init, (jnp.moveaxis(states, 1, 0), jnp.moveaxis(decay, 1, 0)))
    return jnp.moveaxis(prev, 0, 1)


def ssd_scan(xs, b_in, c_in, dt, a_log, d_skip):
    bsz, t = xs.shape[:2]
    nc = t // CHUNK
    dtype = xs.dtype
    x = xs.reshape(bsz, nc, CHUNK, SSD_GROUPS, SSD_HPG, SSD_HEAD_DIM)
    bm = b_in.reshape(bsz, nc, CHUNK, SSD_GROUPS, SSD_STATE)
    cm = c_in.reshape(bsz, nc, CHUNK, SSD_GROUPS, SSD_STATE)
    dtc = dt.reshape(bsz, nc, CHUNK, SSD_GROUPS, SSD_HPG)
    a = -jnp.exp(a_log.astype(jnp.float32)).reshape(SSD_GROUPS, SSD_HPG)
    cum = jnp.cumsum(dtc * a, axis=2)
    xdt = (x * dtc[..., None]).astype(dtype)
    cum_t = jnp.moveaxis(cum, 2, -1)
    causal = jnp.tril(jnp.ones((CHUNK, CHUNK), dtype=bool))
    decay = jnp.exp(jnp.where(causal, cum_t[..., :, None] - cum_t[..., None, :], -jnp.inf))
    cb = jnp.einsum("bclgn,bcsgn->bcgls", cm, bm).astype(jnp.float32)
    w = (cb[:, :, :, None] * decay).astype(dtype)
    y_diag = jnp.einsum("bcghls,bcsghp->bclghp", w, xdt)
    to_end = jnp.exp(cum[:, :, -1:] - cum)
    states = jnp.einsum("bclgn,bclghp->bcghpn", bm, (xdt * to_end[..., None]).astype(dtype))
    chunk_decay = jnp.exp(cum[:, :, -1]).astype(dtype)[..., None, None]
    prev = chunk_state_scan(states, chunk_decay)
    y_off = jnp.einsum("bclgn,bcghpn->bclghp", cm, prev) * jnp.exp(cum)[..., None].astype(dtype)
    y = y_diag + y_off + x * d_skip.reshape(SSD_GROUPS, SSD_HPG, 1).astype(dtype)
    return y.reshape(bsz, t, SSD_WIDTH)


def hgrn2_scan(q, f_raw, v, lb):
    bsz, t = q.shape[:2]
    nc = t // CHUNK
    dtype = q.dtype
    shp = (bsz, nc, CHUNK, HGRN_HEADS, HGRN_KDIM)
    lbh = lb.astype(jnp.float32).reshape(HGRN_HEADS, HGRN_KDIM)
    f = lbh + (1.0 - lbh) * jax.nn.sigmoid(f_raw.astype(jnp.float32))
    k = (1.0 - f).reshape(shp)
    cum = jnp.cumsum(jnp.log(f).reshape(shp), axis=2)
    qf = jax.nn.silu(q.astype(jnp.float32)).reshape(shp)
    vc = v.reshape(bsz, nc, CHUNK, HGRN_HEADS, HGRN_VDIM)
    mid = cum[:, :, CHUNK // 2 - 1:CHUNK // 2]
    q_rel = (qf * jnp.exp(cum - mid)).astype(dtype)
    k_rel = (k * jnp.exp(mid - cum)).astype(dtype)
    causal = jnp.tril(jnp.ones((CHUNK, CHUNK), dtype=bool))
    att = jnp.where(causal, jnp.einsum("bclhk,bcshk->bchls", q_rel, k_rel), 0)
    o_intra = jnp.einsum("bchls,bcshv->bclhv", att, vc)
    k_end = (k * jnp.exp(cum[:, :, -1:] - cum)).astype(dtype)
    states = jnp.einsum("bclhk,bclhv->bchkv", k_end, vc)
    chunk_decay = jnp.exp(cum[:, :, -1]).astype(dtype)[..., None]
    prev = chunk_state_scan(states, chunk_decay)
    o_inter = jnp.einsum("bclhk,bchkv->bclhv", (qf * jnp.exp(cum)).astype(dtype), prev)
    return (o_intra + o_inter).reshape(bsz, t, HGRN_HEADS, HGRN_VDIM)


def ssd_hgrn_mixer(hn, w_in, conv_w, conv_b, dt_bias, a_log, d_skip, ssd_norm_w, lb, hgrn_norm_w, w_out):
    bsz, t, _ = hn.shape
    z, xbc, dt_raw, q, f_raw, v, g = jnp.split(hn @ w_in, AB_SPLITS, axis=-1)
    xbc = jax.nn.silu(causal_dwconv(xbc, conv_w, conv_b))
    xs, b_in, c_in = jnp.split(xbc, [SSD_WIDTH, SSD_WIDTH + SSD_GROUPS * SSD_STATE], axis=-1)
    dt = jax.nn.softplus(dt_raw.astype(jnp.float32) + dt_bias.astype(jnp.float32))
    y_a = ssd_scan(xs, b_in, c_in, dt, a_log, d_skip) * jax.nn.silu(z)
    grp = SSD_WIDTH // SSD_GROUPS
    y_a = rmsnorm(y_a.reshape(bsz, t, SSD_GROUPS, grp), ssd_norm_w.reshape(SSD_GROUPS, grp)).reshape(bsz, t, SSD_WIDTH)
    y_b = hgrn2_scan(q.reshape(bsz, t, HGRN_HEADS, HGRN_KDIM), f_raw.reshape(bsz, t, HGRN_HEADS, HGRN_KDIM),
                     v.reshape(bsz, t, HGRN_HEADS, HGRN_VDIM), lb)
    y_b = rmsnorm(y_b, hgrn_norm_w.reshape(HGRN_HEADS, HGRN_VDIM)).reshape(bsz, t, HGRN_WIDTH) * jax.nn.silu(g)
    return jnp.concatenate([y_a, y_b], axis=-1) @ w_out


def fox_attention(hn, w_in, b_f, w_out):
    bsz, t, _ = hn.shape
    q, k, v, f_raw = jnp.split(hn @ w_in, [FOX_WIDTH, 2 * FOX_WIDTH, 3 * FOX_WIDTH], axis=-1)
    q = q.reshape(bsz, t, FOX_HEADS, FOX_HEAD_DIM)
    k = k.reshape(bsz, t, FOX_HEADS, FOX_HEAD_DIM)
    v = v.reshape(bsz, t, FOX_HEADS, FOX_HEAD_DIM)
    log_f = jax.nn.log_sigmoid(f_raw.astype(jnp.float32) + b_f.astype(jnp.float32))
    dcum = jnp.swapaxes(jnp.cumsum(log_f, axis=1), 1, 2)
    scale = FOX_HEAD_DIM ** -0.5
    q_idx = jnp.arange(Q_BLOCK)
    outs = []
    for blk in range(t // Q_BLOCK):
        s0 = blk * Q_BLOCK
        s1 = s0 + Q_BLOCK
        logits = jnp.einsum("bqhd,bkhd->bhqk", q[:, s0:s1], k[:, :s1]).astype(jnp.float32) * scale
        logits = logits + dcum[:, :, s0:s1, None] - dcum[:, :, None, :s1]
        mask = (s0 + q_idx)[:, None] >= jnp.arange(s1)[None, :]
        probs = jax.nn.softmax(jnp.where(mask, logits, -jnp.inf), axis=-1).astype(v.dtype)
        outs.append(jnp.einsum("bhqk,bkhd->bqhd", probs, v[:, :s1]))
    o = jnp.concatenate(outs, axis=1).reshape(bsz, t, FOX_WIDTH)
    return o @ w_out


def ple_add(h, p_i, gate_norm_w, w_gate, w_up, post_norm_w):
    emb = rmsnorm(p_i @ w_up, post_norm_w)
    gate = jax.nn.sigmoid(rmsnorm(h, gate_norm_w) @ w_gate)
    return h + emb * gate


def setup_inputs(seed: int = 0) -> dict:
    key = jax.random.key(seed)
    ks = iter(jax.random.split(key, 40))
    f32 = jnp.float32
    D = D_MODEL

    def nrm(shape, scale):
        return jax.random.normal(next(ks), shape, f32) * scale

    def gain(shape):
        return 1.0 + 0.05 * jax.random.normal(next(ks), shape, f32)

    x = nrm((BATCH, SEQ, D), 1.0)
    p = nrm((DEPTH, BATCH, SEQ, PLE_DIM), 1.0)
    ffn1_norm = gain((DEPTH, D))
    ffn1_w_in = nrm((DEPTH, D, 2 * D_FF), D ** -0.5)
    ffn1_w_out = nrm((DEPTH, D_FF, D), D_FF ** -0.5)
    mix_norm = gain((DEPTH, D))
    ab_w_in = nrm((N_EVEN, D, AB_IN), D ** -0.5)
    ssd_conv_w = nrm((N_EVEN, SSD_CONV, SSD_CONV_DIM), SSD_CONV ** -0.5)
    ssd_conv_b = nrm((N_EVEN, SSD_CONV_DIM), 0.02)
    dt0 = jnp.exp(jax.random.uniform(next(ks), (N_EVEN, SSD_HEADS), f32, math.log(1e-3), math.log(1e-1)))
    ssd_dt_bias = dt0 + jnp.log(-jnp.expm1(-dt0))
    ssd_a_log = jnp.log(jax.random.uniform(next(ks), (N_EVEN, SSD_HEADS), f32, 1.0, 16.0))
    ssd_d = gain((N_EVEN, SSD_HEADS))
    ssd_norm = gain((N_EVEN, SSD_WIDTH))
    hgrn_lb_logits = nrm((DEPTH + 1, HGRN_WIDTH), 0.1)
    hgrn_norm = gain((N_EVEN, HGRN_WIDTH))
    ab_w_out = nrm((N_EVEN, AB_OUT, D), AB_OUT ** -0.5)
    fox_w_in = nrm((N_ODD, D, FOX_IN), D ** -0.5)
    fox_b_f = 2.0 + nrm((N_ODD, FOX_HEADS), 0.5)
    fox_w_out = nrm((N_ODD, FOX_WIDTH, D), FOX_WIDTH ** -0.5)
    ffn2_norm = gain((DEPTH, D))
    ffn2_w_in = nrm((DEPTH, D, 2 * D_FF), D ** -0.5)
    ffn2_w_out = nrm((DEPTH, D_FF, D), D_FF ** -0.5)
    ple_gate_norm = gain((DEPTH, D))
    ple_w_gate = nrm((DEPTH, D, D), D ** -0.5)
    ple_w_up = nrm((DEPTH, PLE_DIM, D), PLE_DIM ** -0.5)
    ple_norm = gain((DEPTH, D))
    final_norm = gain((D,))
    return {
        "x": x, "p": p,
        "ffn1_norm": ffn1_norm, "ffn1_w_in": ffn1_w_in, "ffn1_w_out": ffn1_w_out,
        "mix_norm": mix_norm, "ab_w_in": ab_w_in, "ssd_conv_w": ssd_conv_w, "ssd_conv_b": ssd_conv_b,
        "ssd_dt_bias": ssd_dt_bias, "ssd_a_log": ssd_a_log, "ssd_d": ssd_d, "ssd_norm": ssd_norm,
        "hgrn_lb_logits": hgrn_lb_logits, "hgrn_norm": hgrn_norm, "ab_w_out": ab_w_out,
        "fox_w_in": fox_w_in, "fox_b_f": fox_b_f, "fox_w_out": fox_w_out,
        "ffn2_norm": ffn2_norm, "ffn2_w_in": ffn2_w_in, "ffn2_w_out": ffn2_w_out,
        "ple_gate_norm": ple_gate_norm, "ple_w_gate": ple_w_gate, "ple_w_up": ple_w_up, "ple_norm": ple_norm,
        "final_norm": final_norm,
    }


def reference(x, p, ffn1_norm, ffn1_w_in, ffn1_w_out, mix_norm, ab_w_in, ssd_conv_w, ssd_conv_b,
              ssd_dt_bias, ssd_a_log, ssd_d, ssd_norm, hgrn_lb_logits, hgrn_norm, ab_w_out,
              fox_w_in, fox_b_f, fox_w_out, ffn2_norm, ffn2_w_in, ffn2_w_out,
              ple_gate_norm, ple_w_gate, ple_w_up, ple_norm, final_norm):
    lb_all = jnp.cumsum(jax.nn.softmax(hgrn_lb_logits.astype(jnp.float32), axis=0), axis=0)
    h = x
    for i in range(DEPTH):
        h = swiglu_half(h, ffn1_norm[i], ffn1_w_in[i], ffn1_w_out[i])
        hn = rmsnorm(h, mix_norm[i])
        j = i // 2
        if i % 2 == 0:
            h = h + ssd_hgrn_mixer(hn, ab_w_in[j], ssd_conv_w[j], ssd_conv_b[j], ssd_dt_bias[j], ssd_a_log[j],
                                   ssd_d[j], ssd_norm[j], lb_all[i], hgrn_norm[j], ab_w_out[j])
        else:
            h = h + fox_attention(hn, fox_w_in[j], fox_b_f[j], fox_w_out[j])
        h = swiglu_half(h, ffn2_norm[i], ffn2_w_in[i], ffn2_w_out[i])
        h = ple_add(h, p[i], ple_gate_norm[i], ple_w_gate[i], ple_w_up[i], ple_norm[i])
    return rmsnorm(h, final_norm)
```

```python
import functools

import jax
import jax.numpy as jnp
from jax import lax
from jax.experimental import pallas as pl
from jax.experimental.pallas import tpu as pltpu

F32 = jnp.float32
BF16 = jnp.bfloat16
EPS = 1e-6

D_MODEL = 2048
D_FF = 5504
D_FF_PAD = 5632
SSD_HEADS = 32
SSD_HEAD_DIM = 64
SSD_GROUPS = 4
SSD_STATE = 128
SSD_GROUP_WIDTH = 512
SSD_CONV = 4
HGRN_HEADS = 16
HGRN_DIM = 128
FOX_HEADS = 16
FOX_DIM = 128
LANES = 128
CONV_HALO = 8

SSD_ROWS = 128
HGRN_ROWS = 64
FOX_BLOCK = 512
VMEM_LIMIT = 56 * 1024 * 1024


def _params(n_axes, vmem=VMEM_LIMIT):
    return pltpu.CompilerParams(dimension_semantics=("arbitrary",) * n_axes, vmem_limit_bytes=vmem)


def _silu(x):
    return x * jax.nn.sigmoid(x)


def _softplus(x):
    return jnp.maximum(x, 0.0) + jnp.log1p(jnp.exp(-jnp.abs(x)))


def _dot(a, b):
    return jnp.dot(a, b, preferred_element_type=F32)


def _dot_nt(a, b):
    return lax.dot_general(a, b, (((1,), (1,)), ((), ())), preferred_element_type=F32)


def _dot_f32(a, b):
    return jnp.dot(a, b, preferred_element_type=F32, precision=lax.Precision.HIGHEST)


def _tril(n):
    r = lax.broadcasted_iota(jnp.int32, (n, n), 0)
    c = lax.broadcasted_iota(jnp.int32, (n, n), 1)
    return r >= c


def _rmsnorm_kernel(h_ref, w_ref, o_ref):
    x = h_ref[...]
    y = x * lax.rsqrt(jnp.mean(x * x, axis=-1, keepdims=True) + EPS)
    o_ref[...] = (y * w_ref[...]).astype(o_ref.dtype)


def rmsnorm(h, w, out_dtype, tm=512):
    m, d = h.shape
    return pl.pallas_call(
        _rmsnorm_kernel,
        grid=(m // tm,),
        in_specs=[pl.BlockSpec((tm, d), lambda i: (i, 0)), pl.BlockSpec((1, d), lambda i: (0, 0))],
        out_specs=pl.BlockSpec((tm, d), lambda i: (i, 0)),
        out_shape=jax.ShapeDtypeStruct((m, d), out_dtype),
        compiler_params=_params(1),
        name="rmsnorm",
    )(h, w.reshape(1, d))


def _mm_kernel(x_ref, w_ref, o_ref):
    o_ref[...] = _dot(x_ref[...], w_ref[...]).astype(o_ref.dtype)


def matmul(x, w, out_dtype, tm=1024, tn=1024):
    m, k = x.shape
    n = w.shape[1]
    tn = min(tn, n)
    return pl.pallas_call(
        _mm_kernel,
        grid=(n // tn, m // tm),
        in_specs=[pl.BlockSpec((tm, k), lambda j, i: (i, 0)), pl.BlockSpec((k, tn), lambda j, i: (0, j))],
        out_specs=pl.BlockSpec((tm, tn), lambda j, i: (i, j)),
        out_shape=jax.ShapeDtypeStruct((m, n), out_dtype),
        compiler_params=_params(2),
        name="matmul",
    )(x, w)


def _ffn_in_kernel(x_ref, wg_ref, wu_ref, o_ref):
    x = x_ref[...]
    gate = _dot(x, wg_ref[...])
    up = _dot(x, wu_ref[...])
    o_ref[...] = (_silu(gate) * up).astype(o_ref.dtype)


def ffn_in(x, wg, wu, tm=1024, tn=512):
    m, k = x.shape
    n = wg.shape[1]
    return pl.pallas_call(
        _ffn_in_kernel,
        grid=(n // tn, m // tm),
        in_specs=[pl.BlockSpec((tm, k), lambda j, i: (i, 0)),
                  pl.BlockSpec((k, tn), lambda j, i: (0, j)),
                  pl.BlockSpec((k, tn), lambda j, i: (0, j))],
        out_specs=pl.BlockSpec((tm, tn), lambda j, i: (i, j)),
        out_shape=jax.ShapeDtypeStruct((m, n), BF16),
        compiler_params=_params(2),
        name="ffn_in",
    )(x, wg, wu)


def _mm_resid_kernel(*refs, n_pairs, scale):
    h_ref = refs[0]
    o_ref = refs[-1]
    acc = _dot(refs[1][...], refs[2][...])
    for p in range(1, n_pairs):
        acc = acc + _dot(refs[1 + 2 * p][...], refs[2 + 2 * p][...])
    if scale != 1.0:
        acc = scale * acc
    o_ref[...] = h_ref[...] + acc


def matmul_residual(h, pairs, scale=1.0, tm=1024, tn=512):
    m, n = h.shape
    in_specs = [pl.BlockSpec((tm, tn), lambda j, i: (i, j))]
    args = [h]
    for a, w in pairs:
        k = a.shape[1]
        in_specs.append(pl.BlockSpec((tm, k), lambda j, i: (i, 0)))
        in_specs.append(pl.BlockSpec((k, tn), lambda j, i: (0, j)))
        args += [a, w]
    return pl.pallas_call(
        functools.partial(_mm_resid_kernel, n_pairs=len(pairs), scale=scale),
        grid=(n // tn, m // tm),
        in_specs=in_specs,
        out_specs=pl.BlockSpec((tm, tn), lambda j, i: (i, j)),
        out_shape=jax.ShapeDtypeStruct((m, n), F32),
        compiler_params=_params(2),
        name="matmul_residual",
    )(*args)


def _ple_kernel(h_ref, hg_ref, p_ref, wg_ref, wu_ref, pn_ref, nn_ref, *out_refs, emit_h):
    gate = jax.nn.sigmoid(_dot(hg_ref[...], wg_ref[...]))
    e = _dot(p_ref[...].astype(BF16), wu_ref[...])
    emb = e * lax.rsqrt(jnp.mean(e * e, axis=-1, keepdims=True) + EPS) * pn_ref[...]
    h = h_ref[...] + emb * gate
    if emit_h:
        out_refs[0][...] = h
    n_ref = out_refs[-1]
    y = h * lax.rsqrt(jnp.mean(h * h, axis=-1, keepdims=True) + EPS)
    n_ref[...] = (y * nn_ref[...]).astype(n_ref.dtype)


def ple_add(h, hg, p, w_gate, w_up, post_norm_w, next_norm_w, final, tm=256):
    m, d = h.shape
    pd = p.shape[1]
    row = lambda i: (i, 0)
    fixed = lambda i: (0, 0)
    out_shape = [jax.ShapeDtypeStruct((m, d), F32 if final else BF16)]
    out_specs = [pl.BlockSpec((tm, d), row)]
    if not final:
        out_shape = [jax.ShapeDtypeStruct((m, d), F32)] + out_shape
        out_specs = [pl.BlockSpec((tm, d), row)] + out_specs
    return pl.pallas_call(
        functools.partial(_ple_kernel, emit_h=not final),
        grid=(m // tm,),
        in_specs=[pl.BlockSpec((tm, d), row), pl.BlockSpec((tm, d), row), pl.BlockSpec((tm, pd), row),
                  pl.BlockSpec((d, d), fixed), pl.BlockSpec((pd, d), fixed),
                  pl.BlockSpec((1, d), fixed), pl.BlockSpec((1, d), fixed)],
        out_specs=out_specs,
        out_shape=out_shape,
        compiler_params=_params(1),
        name="ple_add",
    )(h, hg, p, w_gate, w_up, post_norm_w.reshape(1, d), next_norm_w.reshape(1, d))


def _ssd_kernel(z_ref, xs_ref, bc_ref, dt_ref, cwx_ref, cbx_ref, cwbc_ref, cbbc_ref, dtb_ref, alog_ref,
                dskip_ref, nw_ref, o_ref, xpad_ref, bcpad_ref, st_ref):
    rows = SSD_ROWS

    @pl.when(pl.program_id(0) == 0)
    def _():
        xpad_ref[0:CONV_HALO, :] = jnp.zeros((CONV_HALO, xpad_ref.shape[1]), F32)
        bcpad_ref[0:CONV_HALO, :] = jnp.zeros((CONV_HALO, bcpad_ref.shape[1]), F32)
        st_ref[...] = jnp.zeros(st_ref.shape, F32)

    def conv_silu(pad_ref, raw_ref, w_ref, b_ref):
        pad_ref[CONV_HALO:CONV_HALO + rows, :] = raw_ref[...].astype(F32)
        acc = b_ref[...] + pad_ref[pl.ds(CONV_HALO - SSD_CONV + 1, rows), :] * w_ref[0:1, :]
        for k in range(1, SSD_CONV):
            acc = acc + pad_ref[pl.ds(CONV_HALO - SSD_CONV + 1 + k, rows), :] * w_ref[k:k + 1, :]
        pad_ref[0:CONV_HALO, :] = pad_ref[rows:rows + CONV_HALO, :]
        return _silu(acc)

    xs = conv_silu(xpad_ref, xs_ref, cwx_ref, cbx_ref)
    bc = conv_silu(bcpad_ref, bc_ref, cwbc_ref, cbbc_ref)

    dt = _softplus(dt_ref[...] + dtb_ref[...])
    a = -jnp.exp(alog_ref[...])
    causal = _tril(rows)
    cum = _dot_f32(causal.astype(F32), dt * a)
    cum_t = cum.T
    low_half = lax.broadcasted_iota(jnp.int32, (rows, LANES), 1) < SSD_HEAD_DIM

    def per_pair(v, h0):
        return jnp.where(low_half, v[:, h0:h0 + 1], v[:, h0 + 1:h0 + 2])

    pairs_per_group = SSD_GROUP_WIDTH // LANES
    for g in range(SSD_GROUPS):
        b_g = bc[:, g * SSD_STATE:(g + 1) * SSD_STATE]
        c_g = bc[:, (SSD_GROUPS + g) * SSD_STATE:(SSD_GROUPS + g + 1) * SSD_STATE].astype(BF16)
        cb = _dot_nt(c_g, b_g.astype(BF16))
        st = st_ref[g]
        y_off = _dot(c_g, st.astype(BF16))
        y_tiles, xw_tiles, dec_tiles = [], [], []
        for jj in range(pairs_per_group):
            j = g * pairs_per_group + jj
            h0 = 2 * j
            sl = slice(j * LANES, (j + 1) * LANES)
            x_p = xs[:, sl]
            cum_p = per_pair(cum, h0)
            xdt = x_p * per_pair(dt, h0)
            xdt_b = xdt.astype(BF16)
            halves = []
            for h in (h0, h0 + 1):
                diff = cum[:, h:h + 1] - cum_t[h:h + 1, :]
                decay = jnp.exp(jnp.where(causal, diff, -jnp.inf))
                halves.append(_dot((cb * decay).astype(BF16), xdt_b))
            y_diag = jnp.where(low_half, halves[0], halves[1])
            cum_last = cum_p[rows - 1:rows, :]
            xw_tiles.append((xdt * jnp.exp(cum_last - cum_p)).astype(BF16))
            dec_tiles.append(jnp.exp(cum_last))
            y = y_diag + y_off[:, jj * LANES:(jj + 1) * LANES] * jnp.exp(cum_p) + x_p * dskip_ref[:, sl]
            y_tiles.append(y * _silu(z_ref[:, sl].astype(F32)))
        xw = jnp.concatenate(xw_tiles, axis=1)
        st_ref[g] = st * jnp.concatenate(dec_tiles, axis=1) + _dot(b_g.T.astype(BF16), xw)
        ss = jnp.sum(y_tiles[0] * y_tiles[0], axis=-1, keepdims=True)
        for t in y_tiles[1:]:
            ss = ss + jnp.sum(t * t, axis=-1, keepdims=True)
        inv = lax.rsqrt(ss * (1.0 / SSD_GROUP_WIDTH) + EPS)
        for jj in range(pairs_per_group):
            sl = slice((g * pairs_per_group + jj) * LANES, (g * pairs_per_group + jj + 1) * LANES)
            o_ref[:, sl] = (y_tiles[jj] * inv * nw_ref[:, sl]).astype(o_ref.dtype)


def ssd_mixer(proj, dt_raw, conv_wx, conv_bx, conv_wbc, conv_bbc, dt_bias, a_log, d_skip_x, norm_w):
    t = proj.shape[0]
    rows = SSD_ROWS
    d = D_MODEL
    nbc = 2 * SSD_GROUPS * SSD_STATE
    fixed = lambda i: (0, 0)
    return pl.pallas_call(
        _ssd_kernel,
        grid=(t // rows,),
        in_specs=[pl.BlockSpec((rows, d), lambda i: (i, 0)),
                  pl.BlockSpec((rows, d), lambda i: (i, 1)),
                  pl.BlockSpec((rows, nbc), lambda i: (i, 5 * d // nbc)),
                  pl.BlockSpec((rows, LANES), lambda i: (i, 0)),
                  pl.BlockSpec((SSD_CONV, d), fixed), pl.BlockSpec((1, d), fixed),
                  pl.BlockSpec((SSD_CONV, nbc), fixed), pl.BlockSpec((1, nbc), fixed),
                  pl.BlockSpec((1, LANES), fixed), pl.BlockSpec((1, LANES), fixed),
                  pl.BlockSpec((1, d), fixed), pl.BlockSpec((1, d), fixed)],
        out_specs=pl.BlockSpec((rows, d), lambda i: (i, 0)),
        out_shape=jax.ShapeDtypeStruct((t, d), BF16),
        scratch_shapes=[pltpu.VMEM((rows + CONV_HALO, d), F32),
                        pltpu.VMEM((rows + CONV_HALO, nbc), F32),
                        pltpu.VMEM((SSD_GROUPS, SSD_STATE, SSD_GROUP_WIDTH), F32)],
        compiler_params=_params(1),
        name="ssd_mixer",
    )(proj, proj, proj, dt_raw, conv_wx, conv_bx, conv_wbc, conv_bbc, dt_bias, a_log, d_skip_x, norm_w)


def _hgrn_kernel(q_ref, v_ref, g_ref, f_ref, lb_ref, nw_ref, o_ref, st_ref):
    rows = HGRN_ROWS

    @pl.when(pl.program_id(0) == 0)
    def _():
        st_ref[...] = jnp.zeros(st_ref.shape, F32)

    lb = lb_ref[...]
    f = lb + (1.0 - lb) * jax.nn.sigmoid(f_ref[...])
    k = 1.0 - f
    causal = _tril(rows)
    cum = _dot_f32(causal.astype(F32), jnp.log(f))
    qf = _silu(q_ref[...].astype(F32))
    mid = cum[rows // 2 - 1:rows // 2, :]
    last = cum[rows - 1:rows, :]
    q_rel = (qf * jnp.exp(cum - mid)).astype(BF16)
    k_rel = (k * jnp.exp(mid - cum)).astype(BF16)
    k_end = (k * jnp.exp(last - cum)).astype(BF16)
    q_dec = (qf * jnp.exp(cum)).astype(BF16)
    chunk_decay = jnp.exp(last)
    for h in range(HGRN_HEADS):
        sl = slice(h * HGRN_DIM, (h + 1) * HGRN_DIM)
        v_h = v_ref[:, sl]
        att = jnp.where(causal, _dot_nt(q_rel[:, sl], k_rel[:, sl]), 0.0)
        st = st_ref[h]
        o = _dot(att.astype(BF16), v_h) + _dot_nt(q_dec[:, sl], st.astype(BF16))
        st_ref[h] = st * chunk_decay[:, sl] + _dot(v_h.astype(F32).T.astype(BF16), k_end[:, sl])
        on = o * lax.rsqrt(jnp.mean(o * o, axis=-1, keepdims=True) + EPS) * nw_ref[:, sl]
        o_ref[:, sl] = (on * _silu(g_ref[:, sl].astype(F32))).astype(o_ref.dtype)


def hgrn_mixer(proj, f_raw, lb, norm_w):
    t = proj.shape[0]
    rows = HGRN_ROWS
    d = D_MODEL
    fixed = lambda i: (0, 0)
    return pl.pallas_call(
        _hgrn_kernel,
        grid=(t // rows,),
        in_specs=[pl.BlockSpec((rows, d), lambda i: (i, 2)),
                  pl.BlockSpec((rows, d), lambda i: (i, 3)),
                  pl.BlockSpec((rows, d), lambda i: (i, 4)),
                  pl.BlockSpec((rows, d), lambda i: (i, 0)),
                  pl.BlockSpec((1, d), fixed), pl.BlockSpec((1, d), fixed)],
        out_specs=pl.BlockSpec((rows, d), lambda i: (i, 0)),
        out_shape=jax.ShapeDtypeStruct((t, d), BF16),
        scratch_shapes=[pltpu.VMEM((HGRN_HEADS, HGRN_DIM, HGRN_DIM), F32)],
        compiler_params=_params(1),
        name="hgrn_mixer",
    )(proj, proj, proj, f_raw, lb, norm_w)


def _logf_cumsum_kernel(f_ref, b_ref, o_ref, ot_ref, carry_ref):
    rows = f_ref.shape[0]

    @pl.when(pl.program_id(0) == 0)
    def _():
        carry_ref[...] = jnp.zeros(carry_ref.shape, F32)

    log_f = -_softplus(-(f_ref[...] + b_ref[...]))
    c = _dot_f32(_tril(rows).astype(F32), log_f) + carry_ref[...]
    carry_ref[...] = c[rows - 1:rows, :]
    o_ref[...] = c
    ot_ref[...] = c.T


def logf_cumsum(f_raw, b_f, rows=512):
    t = f_raw.shape[0]
    return pl.pallas_call(
        _logf_cumsum_kernel,
        grid=(t // rows,),
        in_specs=[pl.BlockSpec((rows, LANES), lambda i: (i, 0)), pl.BlockSpec((1, LANES), lambda i: (0, 0))],
        out_specs=[pl.BlockSpec((rows, LANES), lambda i: (i, 0)), pl.BlockSpec((LANES, rows), lambda i: (0, i))],
        out_shape=[jax.ShapeDtypeStruct((t, LANES), F32), jax.ShapeDtypeStruct((LANES, t), F32)],
        scratch_shapes=[pltpu.VMEM((1, LANES), F32)],
        compiler_params=_params(1),
        name="logf_cumsum",
    )(f_raw, b_f)


def _fox_kernel(q_ref, k_ref, v_ref, dq_ref, dk_ref, o_ref):
    blk = FOX_BLOCK
    h = pl.program_id(0)
    qi = pl.program_id(1)
    q = q_ref[...]
    scale = FOX_DIM ** -0.5
    lane = lax.broadcasted_iota(jnp.int32, (blk, LANES), 1)
    dq = jnp.sum(jnp.where(lane == h, dq_ref[...], 0.0), axis=-1, keepdims=True)
    causal = _tril(blk)

    def step(ki, carry, diagonal):
        m, l, acc = carry
        start = pl.multiple_of(ki * blk, blk)
        k_b = k_ref[pl.ds(start, blk), :]
        v_b = v_ref[pl.ds(start, blk), :]
        s = _dot_nt(q, k_b) * scale + (dq - dk_ref[:, pl.ds(start, blk)])
        if diagonal:
            s = jnp.where(causal, s, -jnp.inf)
        m_new = jnp.maximum(m, jnp.max(s, axis=-1, keepdims=True))
        alpha = jnp.exp(m - m_new)
        p = jnp.exp(s - m_new)
        l = alpha * l + jnp.sum(p, axis=-1, keepdims=True)
        acc = alpha * acc + _dot(p.astype(BF16), v_b)
        return m_new, l, acc

    init = (jnp.full((blk, 1), -jnp.inf, F32), jnp.zeros((blk, 1), F32), jnp.zeros((blk, FOX_DIM), F32))
    carry = lax.fori_loop(0, qi, lambda ki, c: step(ki, c, False), init)
    _, l, acc = step(qi, carry, True)
    o_ref[...] = (acc / l).astype(o_ref.dtype)


def fox_attention(qkv, dcum, dcum_t):
    t = qkv.shape[0]
    blk = FOX_BLOCK
    return pl.pallas_call(
        _fox_kernel,
        grid=(FOX_HEADS, t // blk),
        in_specs=[pl.BlockSpec((blk, FOX_DIM), lambda h, i: (i, h)),
                  pl.BlockSpec((t, FOX_DIM), lambda h, i: (0, FOX_HEADS + h)),
                  pl.BlockSpec((t, FOX_DIM), lambda h, i: (0, 2 * FOX_HEADS + h)),
                  pl.BlockSpec((blk, LANES), lambda h, i: (i, 0)),
                  pl.BlockSpec((None, 1, t), lambda h, i: (h, 0, 0))],
        out_specs=pl.BlockSpec((blk, FOX_DIM), lambda h, i: (i, h)),
        out_shape=jax.ShapeDtypeStruct((t, FOX_HEADS * FOX_DIM), BF16),
        compiler_params=_params(2),
        name="fox_attention",
    )(qkv, qkv, qkv, dcum, dcum_t)


def _pad_cols(w, n):
    return jnp.pad(w, ((0, 0), (0, n - w.shape[1])))


def _ffn_weights(w_in, w_out):
    wg = _pad_cols(w_in[:, :D_FF], D_FF_PAD).astype(BF16)
    wu = _pad_cols(w_in[:, D_FF:], D_FF_PAD).astype(BF16)
    wo = jnp.pad(w_out, ((0, D_FF_PAD - D_FF), (0, 0))).astype(BF16)
    return wg, wu, wo


def _swiglu_half(h, hn, w_in, w_out):
    wg, wu, wo = _ffn_weights(w_in, w_out)
    act = ffn_in(hn, wg, wu)
    return matmul_residual(h, [(act, wo)], scale=0.5)


def _ssd_hgrn_layer(h, hn, w_in, conv_w, conv_b, dt_bias, a_log, d_skip, ssd_norm_w, lb, hgrn_norm_w, w_out):
    d = D_MODEL
    nb = SSD_GROUPS * SSD_STATE
    o_xbc, o_dt, o_q = d, 2 * d + 2 * nb, 2 * d + 2 * nb + SSD_HEADS
    z_w = w_in[:, :d]
    xs_w = w_in[:, o_xbc:o_xbc + d]
    bc_w = w_in[:, o_xbc + d:o_dt]
    dt_w = w_in[:, o_dt:o_q]
    q_w, f_w, v_w, g_w = (w_in[:, o_q + i * d:o_q + (i + 1) * d] for i in range(4))
    w_main = jnp.concatenate([z_w, xs_w, q_w, v_w, g_w, bc_w], axis=1).astype(BF16)
    proj = matmul(hn, w_main, BF16)
    f_raw = matmul(hn, f_w.astype(BF16), F32)
    dt_raw = matmul(hn, _pad_cols(dt_w, LANES).astype(BF16), F32)
    pad_heads = lambda v: jnp.pad(v, (0, LANES - SSD_HEADS)).reshape(1, LANES)
    y_a = ssd_mixer(proj, dt_raw, conv_w[:, :d], conv_b[:d].reshape(1, d), conv_w[:, d:],
                    conv_b[d:].reshape(1, 2 * nb), pad_heads(dt_bias), pad_heads(a_log),
                    jnp.repeat(d_skip, SSD_HEAD_DIM).reshape(1, d), ssd_norm_w.reshape(1, d))
    y_b = hgrn_mixer(proj, f_raw, lb.reshape(1, d), hgrn_norm_w.reshape(1, d))
    wo = w_out.astype(BF16)
    return matmul_residual(h, [(y_a, wo[:d]), (y_b, wo[d:])])


def _fox_layer(h, hn, w_in, b_f, w_out):
    d = D_MODEL
    t = h.shape[0]
    qkv = matmul(hn, w_in[:, :3 * d].astype(BF16), BF16)
    f_raw = matmul(hn, _pad_cols(w_in[:, 3 * d:], LANES).astype(BF16), F32)
    b_pad = jnp.pad(b_f, (0, LANES - FOX_HEADS)).reshape(1, LANES)
    dcum, dcum_t = logf_cumsum(f_raw, b_pad)
    o = fox_attention(qkv, dcum, dcum_t[:FOX_HEADS].reshape(FOX_HEADS, 1, t))
    return matmul_residual(h, [(o, w_out.astype(BF16))])


def kernel(x, p, ffn1_norm, ffn1_w_in, ffn1_w_out, mix_norm, ab_w_in, ssd_conv_w, ssd_conv_b, ssd_dt_bias,
           ssd_a_log, ssd_d, ssd_norm, hgrn_lb_logits, hgrn_norm, ab_w_out, fox_w_in, fox_b_f, fox_w_out,
           ffn2_norm, ffn2_w_in, ffn2_w_out, ple_gate_norm, ple_w_gate, ple_w_up, ple_norm, final_norm):
    bsz, t, d = x.shape
    depth = p.shape[0]
    assert bsz == 1 and d == D_MODEL
    lb_all = jnp.cumsum(jax.nn.softmax(hgrn_lb_logits.astype(F32), axis=0), axis=0)
    h = x.reshape(t, d)
    hn = rmsnorm(h, ffn1_norm[0], BF16)
    for i in range(depth):
        j = i // 2
        h = _swiglu_half(h, hn, ffn1_w_in[i], ffn1_w_out[i])
        hn = rmsnorm(h, mix_norm[i], BF16)
        if i % 2 == 0:
            h = _ssd_hgrn_layer(h, hn, ab_w_in[j], ssd_conv_w[j], ssd_conv_b[j], ssd_dt_bias[j], ssd_a_log[j],
                                ssd_d[j], ssd_norm[j], lb_all[i], hgrn_norm[j], ab_w_out[j])
        else:
            h = _fox_layer(h, hn, fox_w_in[j], fox_b_f[j], fox_w_out[j])
        hn = rmsnorm(h, ffn2_norm[i], BF16)
        h = _swiglu_half(h, hn, ffn2_w_in[i], ffn2_w_out[i])
        hg = rmsnorm(h, ple_gate_norm[i], BF16)
        final = i == depth - 1
        next_w = final_norm if final else ffn1_norm[i + 1]
        outs = ple_add(h, hg, p[i].reshape(t, -1), ple_w_gate[i].astype(BF16), ple_w_up[i].astype(BF16),
                       ple_norm[i], next_w, final)
        if final:
            return outs[0].reshape(bsz, t, d)
        h, hn = outs
```

```python
import functools
import math

import jax
import jax.numpy as jnp
from jax import lax
from jax.experimental import pallas as pl
from jax.experimental.pallas import tpu as pltpu

F32 = jnp.float32
BF16 = jnp.bfloat16
EPS = 1e-6

D_MODEL = 2048
D_FF = 5504
SSD_HEADS = 32
SSD_HEAD_DIM = 64
SSD_GROUPS = 4
SSD_STATE = 128
SSD_GROUP_WIDTH = 512
SSD_CONV = 4
HGRN_HEADS = 16
HGRN_DIM = 128
FOX_HEADS = 16
FOX_DIM = 128
LANES = 128
CONV_HALO = 8

SSD_ROWS = 128
HGRN_ROWS = 64
FOX_Q_BLOCK = 512
FOX_K_BLOCK = 1024
FOX_PACK = 2
FFN_TILE = 512
VMEM_LIMIT = 56 * 1024 * 1024
LOG2E = math.log2(math.e)


def _params(n_axes, vmem=VMEM_LIMIT):
    return pltpu.CompilerParams(dimension_semantics=("arbitrary",) * n_axes, vmem_limit_bytes=vmem)


def _silu(x):
    return x * jax.nn.sigmoid(x)


def _softplus(x):
    return jnp.maximum(x, 0.0) + jnp.log1p(jnp.exp(-jnp.abs(x)))


def _dot(a, b):
    return jnp.dot(a, b, preferred_element_type=F32)


def _dot_nt(a, b):
    return lax.dot_general(a, b, (((1,), (1,)), ((), ())), preferred_element_type=F32)


def _dot_f32(a, b):
    return jnp.dot(a, b, preferred_element_type=F32, precision=lax.Precision.HIGHEST)


def _tril(n):
    r = lax.broadcasted_iota(jnp.int32, (n, n), 0)
    c = lax.broadcasted_iota(jnp.int32, (n, n), 1)
    return r >= c


def _weight_window(rows, width, row0, col_of_block):
    return pl.BlockSpec((pl.Element(rows), pl.Element(width)),
                        lambda j, i: (row0, col_of_block(j) * LANES))


def _cast_weights_once(pairs):
    @pl.when(pl.program_id(1) == 0)
    def _():
        for w_ref, wb_ref in pairs:
            wb_ref[...] = w_ref[...].astype(BF16)


def _rmsnorm_kernel(h_ref, w_ref, o_ref):
    x = h_ref[...]
    y = x * lax.rsqrt(jnp.mean(x * x, axis=-1, keepdims=True) + EPS)
    o_ref[...] = (y * w_ref[...]).astype(o_ref.dtype)


def rmsnorm(h, w, out_dtype, tm=512):
    m, d = h.shape
    return pl.pallas_call(
        _rmsnorm_kernel,
        grid=(m // tm,),
        in_specs=[pl.BlockSpec((tm, d), lambda i: (i, 0)), pl.BlockSpec((1, d), lambda i: (0, 0))],
        out_specs=pl.BlockSpec((tm, d), lambda i: (i, 0)),
        out_shape=jax.ShapeDtypeStruct((m, d), out_dtype),
        compiler_params=_params(1),
        name="rmsnorm",
    )(h, w.reshape(1, d))


def _mm_kernel(x_ref, w_ref, o_ref, wb_ref, *, scaled_blocks, scale):
    _cast_weights_once([(w_ref, wb_ref)])
    acc = _dot(x_ref[...], wb_ref[...])
    if scaled_blocks:
        acc = acc * jnp.where(pl.program_id(0) < scaled_blocks, scale, 1.0)
    o_ref[...] = acc.astype(o_ref.dtype)


def matmul(x, w, out_dtype, n_out, col_of_block, tn=1024, tm=1024, scaled_blocks=0, scale=1.0):
    m, k = x.shape
    return pl.pallas_call(
        functools.partial(_mm_kernel, scaled_blocks=scaled_blocks, scale=scale),
        grid=(n_out // tn, m // tm),
        in_specs=[pl.BlockSpec((tm, k), lambda j, i: (i, 0)), _weight_window(k, tn, 0, col_of_block)],
        out_specs=pl.BlockSpec((tm, tn), lambda j, i: (i, j)),
        out_shape=jax.ShapeDtypeStruct((m, n_out), out_dtype),
        scratch_shapes=[pltpu.VMEM((k, tn), BF16)],
        compiler_params=_params(2),
        name="matmul",
    )(x, w)


def _ffn_in_kernel(x_ref, wg_ref, wu_ref, o_ref, wgb_ref, wub_ref):
    _cast_weights_once([(wg_ref, wgb_ref), (wu_ref, wub_ref)])
    x = x_ref[...]
    gate = _dot(x, wgb_ref[...])
    up = _dot(x, wub_ref[...])
    o_ref[...] = (_silu(gate) * up).astype(o_ref.dtype)


def ffn_in(x, w_in, col0, n_out, tn, tm=1024):
    m, k = x.shape
    blocks_per_tile = tn // LANES
    gate_col = lambda j: col0 // LANES + j * blocks_per_tile
    up_col = lambda j: (D_FF + col0) // LANES + j * blocks_per_tile
    return pl.pallas_call(
        _ffn_in_kernel,
        grid=(n_out // tn, m // tm),
        in_specs=[pl.BlockSpec((tm, k), lambda j, i: (i, 0)),
                  _weight_window(k, tn, 0, gate_col), _weight_window(k, tn, 0, up_col)],
        out_specs=pl.BlockSpec((tm, tn), lambda j, i: (i, j)),
        out_shape=jax.ShapeDtypeStruct((m, n_out), BF16),
        scratch_shapes=[pltpu.VMEM((k, tn), BF16), pltpu.VMEM((k, tn), BF16)],
        compiler_params=_params(2),
        name="ffn_in",
    )(x, w_in, w_in)


def _mm_resid_kernel(*refs, n_pairs, scale):
    h_ref = refs[0]
    a_refs = refs[1:1 + n_pairs]
    w_refs = refs[1 + n_pairs:1 + 2 * n_pairs]
    o_ref = refs[1 + 2 * n_pairs]
    wb_refs = refs[2 + 2 * n_pairs:]
    _cast_weights_once(list(zip(w_refs, wb_refs)))
    acc = _dot(a_refs[0][...], wb_refs[0][...])
    for a_ref, wb_ref in zip(a_refs[1:], wb_refs[1:]):
        acc = acc + _dot(a_ref[...], wb_ref[...])
    if scale != 1.0:
        acc = scale * acc
    o_ref[...] = h_ref[...] + acc


def matmul_residual(h, w, pieces, scale=1.0, tm=512, tn=512):
    m, n = h.shape
    blocks_per_tile = tn // LANES
    a_specs = [pl.BlockSpec((tm, a.shape[1]), lambda j, i: (i, 0)) for a, _ in pieces]
    w_specs = [_weight_window(a.shape[1], tn, row0, lambda j: j * blocks_per_tile) for a, row0 in pieces]
    return pl.pallas_call(
        functools.partial(_mm_resid_kernel, n_pairs=len(pieces), scale=scale),
        grid=(n // tn, m // tm),
        in_specs=[pl.BlockSpec((tm, tn), lambda j, i: (i, j))] + a_specs + w_specs,
        out_specs=pl.BlockSpec((tm, tn), lambda j, i: (i, j)),
        out_shape=jax.ShapeDtypeStruct((m, n), F32),
        scratch_shapes=[pltpu.VMEM((a.shape[1], tn), BF16) for a, _ in pieces],
        compiler_params=_params(2),
        name="matmul_residual",
    )(h, *[a for a, _ in pieces], *[w] * len(pieces))


def _ple_kernel(h_ref, hg_ref, p_ref, wg_ref, wu_ref, pn_ref, nn_ref, *refs, emit_h):
    wgb_ref, wub_ref = refs[-2:]
    out_refs = refs[:-2]

    @pl.when(pl.program_id(0) == 0)
    def _():
        wgb_ref[...] = wg_ref[...].astype(BF16)
        wub_ref[...] = wu_ref[...].astype(BF16)

    gate = jax.nn.sigmoid(_dot(hg_ref[...], wgb_ref[...]))
    e = _dot(p_ref[...].astype(BF16), wub_ref[...])
    emb = e * lax.rsqrt(jnp.mean(e * e, axis=-1, keepdims=True) + EPS) * pn_ref[...]
    h = h_ref[...] + emb * gate
    if emit_h:
        out_refs[0][...] = h
    n_ref = out_refs[-1]
    y = h * lax.rsqrt(jnp.mean(h * h, axis=-1, keepdims=True) + EPS)
    n_ref[...] = (y * nn_ref[...]).astype(n_ref.dtype)


def ple_add(h, hg, p, w_gate, w_up, post_norm_w, next_norm_w, final, tm=256):
    m, d = h.shape
    pd = p.shape[1]
    row = lambda i: (i, 0)
    fixed = lambda i: (0, 0)
    resident = functools.partial(pl.BlockSpec, index_map=fixed, pipeline_mode=pl.Buffered(1))
    out_shape = [jax.ShapeDtypeStruct((m, d), F32 if final else BF16)]
    out_specs = [pl.BlockSpec((tm, d), row)]
    if not final:
        out_shape = [jax.ShapeDtypeStruct((m, d), F32)] + out_shape
        out_specs = [pl.BlockSpec((tm, d), row)] + out_specs
    return pl.pallas_call(
        functools.partial(_ple_kernel, emit_h=not final),
        grid=(m // tm,),
        in_specs=[pl.BlockSpec((tm, d), row), pl.BlockSpec((tm, d), row), pl.BlockSpec((tm, pd), row),
                  resident((d, d)), resident((pd, d)),
                  pl.BlockSpec((1, d), fixed), pl.BlockSpec((1, d), fixed)],
        out_specs=out_specs,
        out_shape=out_shape,
        scratch_shapes=[pltpu.VMEM((d, d), BF16), pltpu.VMEM((pd, d), BF16)],
        compiler_params=_params(1),
        name="ple_add",
    )(h, hg, p, w_gate, w_up, post_norm_w.reshape(1, d), next_norm_w.reshape(1, d))


def _ssd_kernel(z_ref, xs_ref, bc_ref, dt_ref, cwx_ref, cbx_ref, cwbc_ref, cbbc_ref, dtb_ref, alog_ref,
                dskip_ref, nw_ref, o_ref, xpad_ref, bcpad_ref, st_ref):
    rows = SSD_ROWS

    @pl.when(pl.program_id(0) == 0)
    def _():
        xpad_ref[0:CONV_HALO, :] = jnp.zeros((CONV_HALO, xpad_ref.shape[1]), F32)
        bcpad_ref[0:CONV_HALO, :] = jnp.zeros((CONV_HALO, bcpad_ref.shape[1]), F32)
        st_ref[...] = jnp.zeros(st_ref.shape, F32)

    def conv_silu(pad_ref, raw_ref, w_ref, b_ref):
        pad_ref[CONV_HALO:CONV_HALO + rows, :] = raw_ref[...].astype(F32)
        acc = b_ref[...] + pad_ref[pl.ds(CONV_HALO - SSD_CONV + 1, rows), :] * w_ref[0:1, :]
        for k in range(1, SSD_CONV):
            acc = acc + pad_ref[pl.ds(CONV_HALO - SSD_CONV + 1 + k, rows), :] * w_ref[k:k + 1, :]
        pad_ref[0:CONV_HALO, :] = pad_ref[rows:rows + CONV_HALO, :]
        return _silu(acc)

    xs = conv_silu(xpad_ref, xs_ref, cwx_ref, cbx_ref)
    bc = conv_silu(bcpad_ref, bc_ref, cwbc_ref, cbbc_ref)

    dt = _softplus(dt_ref[...] + dtb_ref[...])
    a = -jnp.exp(alog_ref[...])
    causal = _tril(rows)
    cum = _dot_f32(causal.astype(F32), dt * a)
    cum_t = cum.T
    low_half = lax.broadcasted_iota(jnp.int32, (rows, LANES), 1) < SSD_HEAD_DIM

    def per_pair(v, h0):
        return jnp.where(low_half, v[:, h0:h0 + 1], v[:, h0 + 1:h0 + 2])

    pairs_per_group = SSD_GROUP_WIDTH // LANES
    for g in range(SSD_GROUPS):
        b_g = bc[:, g * SSD_STATE:(g + 1) * SSD_STATE]
        c_g = bc[:, (SSD_GROUPS + g) * SSD_STATE:(SSD_GROUPS + g + 1) * SSD_STATE].astype(BF16)
        cb = _dot_nt(c_g, b_g.astype(BF16))
        st = st_ref[g]
        y_off = _dot(c_g, st.astype(BF16))
        y_tiles, xw_tiles, dec_tiles = [], [], []
        for jj in range(pairs_per_group):
            j = g * pairs_per_group + jj
            h0 = 2 * j
            sl = slice(j * LANES, (j + 1) * LANES)
            x_p = xs[:, sl]
            cum_p = per_pair(cum, h0)
            xdt = x_p * per_pair(dt, h0)
            xdt_b = xdt.astype(BF16)
            halves = []
            for h in (h0, h0 + 1):
                diff = cum[:, h:h + 1] - cum_t[h:h + 1, :]
                decay = jnp.exp(jnp.where(causal, diff, -jnp.inf))
                halves.append(_dot((cb * decay).astype(BF16), xdt_b))
            y_diag = jnp.where(low_half, halves[0], halves[1])
            cum_last = cum_p[rows - 1:rows, :]
            xw_tiles.append((xdt * jnp.exp(cum_last - cum_p)).astype(BF16))
            dec_tiles.append(jnp.exp(cum_last))
            y = y_diag + y_off[:, jj * LANES:(jj + 1) * LANES] * jnp.exp(cum_p) + x_p * dskip_ref[:, sl]
            y_tiles.append(y * _silu(z_ref[:, sl].astype(F32)))
        xw = jnp.concatenate(xw_tiles, axis=1)
        st_ref[g] = st * jnp.concatenate(dec_tiles, axis=1) + _dot(b_g.T.astype(BF16), xw)
        ss = jnp.sum(y_tiles[0] * y_tiles[0], axis=-1, keepdims=True)
        for t in y_tiles[1:]:
            ss = ss + jnp.sum(t * t, axis=-1, keepdims=True)
        inv = lax.rsqrt(ss * (1.0 / SSD_GROUP_WIDTH) + EPS)
        for jj in range(pairs_per_group):
            sl = slice((g * pairs_per_group + jj) * LANES, (g * pairs_per_group + jj + 1) * LANES)
            o_ref[:, sl] = (y_tiles[jj] * inv * nw_ref[:, sl]).astype(o_ref.dtype)


def ssd_mixer(proj, dt_raw, conv_wx, conv_bx, conv_wbc, conv_bbc, dt_bias, a_log, d_skip_x, norm_w):
    t = proj.shape[0]
    rows = SSD_ROWS
    d = D_MODEL
    nbc = 2 * SSD_GROUPS * SSD_STATE
    fixed = lambda i: (0, 0)
    return pl.pallas_call(
        _ssd_kernel,
        grid=(t // rows,),
        in_specs=[pl.BlockSpec((rows, d), lambda i: (i, 0)),
                  pl.BlockSpec((rows, d), lambda i: (i, 1)),
                  pl.BlockSpec((rows, nbc), lambda i: (i, 2 * d // nbc)),
                  pl.BlockSpec((rows, LANES), lambda i: (i, 0)),
                  pl.BlockSpec((SSD_CONV, d), fixed), pl.BlockSpec((1, d), fixed),
                  pl.BlockSpec((SSD_CONV, nbc), fixed), pl.BlockSpec((1, nbc), fixed),
                  pl.BlockSpec((1, LANES), fixed), pl.BlockSpec((1, LANES), fixed),
                  pl.BlockSpec((1, d), fixed), pl.BlockSpec((1, d), fixed)],
        out_specs=pl.BlockSpec((rows, d), lambda i: (i, 0)),
        out_shape=jax.ShapeDtypeStruct((t, d), BF16),
        scratch_shapes=[pltpu.VMEM((rows + CONV_HALO, d), F32),
                        pltpu.VMEM((rows + CONV_HALO, nbc), F32),
                        pltpu.VMEM((SSD_GROUPS, SSD_STATE, SSD_GROUP_WIDTH), F32)],
        compiler_params=_params(1),
        name="ssd_mixer",
    )(proj, proj, proj, dt_raw, conv_wx, conv_bx, conv_wbc, conv_bbc, dt_bias, a_log, d_skip_x, norm_w)


def _hgrn_kernel(q_ref, v_ref, g_ref, f_ref, lb_ref, nw_ref, o_ref, st_ref):
    rows = HGRN_ROWS

    @pl.when(pl.program_id(0) == 0)
    def _():
        st_ref[...] = jnp.zeros(st_ref.shape, F32)

    lb = lb_ref[...]
    f = lb + (1.0 - lb) * jax.nn.sigmoid(f_ref[...])
    k = 1.0 - f
    causal = _tril(rows)
    cum = _dot_f32(causal.astype(F32), jnp.log(f))
    qf = _silu(q_ref[...].astype(F32))
    mid = cum[rows // 2 - 1:rows // 2, :]
    last = cum[rows - 1:rows, :]
    q_rel = (qf * jnp.exp(cum - mid)).astype(BF16)
    k_rel = (k * jnp.exp(mid - cum)).astype(BF16)
    k_end = (k * jnp.exp(last - cum)).astype(BF16)
    q_dec = (qf * jnp.exp(cum)).astype(BF16)
    chunk_decay = jnp.exp(last)
    for h in range(HGRN_HEADS):
        sl = slice(h * HGRN_DIM, (h + 1) * HGRN_DIM)
        v_h = v_ref[:, sl]
        att = jnp.where(causal, _dot_nt(q_rel[:, sl], k_rel[:, sl]), 0.0)
        st = st_ref[h]
        o = _dot(att.astype(BF16), v_h) + _dot_nt(q_dec[:, sl], st.astype(BF16))
        st_ref[h] = st * chunk_decay[:, sl] + _dot(v_h.astype(F32).T.astype(BF16), k_end[:, sl])
        on = o * lax.rsqrt(jnp.mean(o * o, axis=-1, keepdims=True) + EPS) * nw_ref[:, sl]
        o_ref[:, sl] = (on * _silu(g_ref[:, sl].astype(F32))).astype(o_ref.dtype)


def hgrn_mixer(qvg, f_raw, lb, norm_w):
    t = qvg.shape[0]
    rows = HGRN_ROWS
    d = D_MODEL
    fixed = lambda i: (0, 0)
    return pl.pallas_call(
        _hgrn_kernel,
        grid=(t // rows,),
        in_specs=[pl.BlockSpec((rows, d), lambda i: (i, 0)),
                  pl.BlockSpec((rows, d), lambda i: (i, 1)),
                  pl.BlockSpec((rows, d), lambda i: (i, 2)),
                  pl.BlockSpec((rows, d), lambda i: (i, 0)),
                  pl.BlockSpec((1, d), fixed), pl.BlockSpec((1, d), fixed)],
        out_specs=pl.BlockSpec((rows, d), lambda i: (i, 0)),
        out_shape=jax.ShapeDtypeStruct((t, d), BF16),
        scratch_shapes=[pltpu.VMEM((HGRN_HEADS, HGRN_DIM, HGRN_DIM), F32)],
        compiler_params=_params(1),
        name="hgrn_mixer",
    )(qvg, qvg, qvg, f_raw, lb, norm_w)


def _logf_cumsum_kernel(f_ref, b_ref, o_ref, ot_ref, carry_ref):
    rows = f_ref.shape[0]

    @pl.when(pl.program_id(0) == 0)
    def _():
        carry_ref[...] = jnp.zeros(carry_ref.shape, F32)

    log_f = -_softplus(-(f_ref[...] + b_ref[...]))
    c = _dot_f32(_tril(rows).astype(F32), log_f) + carry_ref[...]
    carry_ref[...] = c[rows - 1:rows, :]
    c2 = c * LOG2E
    o_ref[...] = c2
    ot_ref[...] = c2.T


def logf_cumsum(f_raw, b_f, rows=512):
    t = f_raw.shape[0]
    return pl.pallas_call(
        _logf_cumsum_kernel,
        grid=(t // rows,),
        in_specs=[pl.BlockSpec((rows, LANES), lambda i: (i, 0)), pl.BlockSpec((1, LANES), lambda i: (0, 0))],
        out_specs=[pl.BlockSpec((rows, LANES), lambda i: (i, 0)), pl.BlockSpec((LANES, rows), lambda i: (0, i))],
        out_shape=[jax.ShapeDtypeStruct((t, LANES), F32), jax.ShapeDtypeStruct((LANES, t), F32)],
        scratch_shapes=[pltpu.VMEM((1, LANES), F32)],
        compiler_params=_params(1),
        name="logf_cumsum",
    )(f_raw, b_f)


def _fox_kernel(q_ref, k_ref, v_ref, dq_ref, dk_ref, o_ref):
    tq, tk = FOX_Q_BLOCK, FOX_K_BLOCK
    hp = pl.program_id(0)
    qi = pl.program_id(1)
    lane = lax.broadcasted_iota(jnp.int32, (tq, LANES), 1)
    dq_all = dq_ref[...]
    heads = []
    for hh in range(FOX_PACK):
        sl = slice(hh * FOX_DIM, (hh + 1) * FOX_DIM)
        dq = jnp.sum(jnp.where(lane == hp * FOX_PACK + hh, dq_all, 0.0), axis=-1, keepdims=True)
        heads.append((hh, sl, q_ref[:, sl], dq))
    causal = _tril(tq)

    def step(start, width, carry, diagonal):
        out = []
        for (hh, sl, q, dq), (m, l, acc) in zip(heads, carry):
            k_b = k_ref[pl.ds(start, width), sl]
            v_b = v_ref[pl.ds(start, width), sl]
            s = _dot_nt(q, k_b) - dk_ref[hh, :, pl.ds(start, width)]
            if diagonal:
                s = jnp.where(causal, s, -jnp.inf)
            m_new = jnp.maximum(m, jnp.max(s, axis=-1, keepdims=True) + dq)
            alpha = jnp.exp2(m - m_new)
            p = jnp.exp2(s - (m_new - dq))
            l = alpha * l + jnp.sum(p, axis=-1, keepdims=True)
            acc = alpha * acc + _dot(p.astype(BF16), v_b)
            out.append((m_new, l, acc))
        return tuple(out)

    init = tuple((jnp.full((tq, 1), -jnp.inf, F32), jnp.zeros((tq, 1), F32), jnp.zeros((tq, FOX_DIM), F32))
                 for _ in heads)
    n_wide = lax.shift_right_logical(qi, 1)
    carry = lax.fori_loop(0, n_wide, lambda ki, c: step(pl.multiple_of(ki * tk, tk), tk, c, False), init)
    odd_start = pl.multiple_of(n_wide * tk, tk)
    carry = lax.fori_loop(0, qi & 1, lambda _, c: step(odd_start, tq, c, False), carry)
    carry = step(pl.multiple_of(qi * tq, tq), tq, carry, True)
    for (hh, sl, _, _), (_, l, acc) in zip(heads, carry):
        o_ref[:, sl] = (acc / l).astype(o_ref.dtype)


def fox_attention(qkv, dcum, dcum_t):
    t = qkv.shape[0]
    tq = FOX_Q_BLOCK
    width = FOX_PACK * FOX_DIM
    groups = FOX_HEADS // FOX_PACK
    return pl.pallas_call(
        _fox_kernel,
        grid=(groups, t // tq),
        in_specs=[pl.BlockSpec((tq, width), lambda h, i: (i, h)),
                  pl.BlockSpec((t, width), lambda h, i: (0, groups + h)),
                  pl.BlockSpec((t, width), lambda h, i: (0, 2 * groups + h)),
                  pl.BlockSpec((tq, LANES), lambda h, i: (i, 0)),
                  pl.BlockSpec((FOX_PACK, 1, t), lambda h, i: (h, 0, 0))],
        out_specs=pl.BlockSpec((tq, width), lambda h, i: (i, h)),
        out_shape=jax.ShapeDtypeStruct((t, FOX_HEADS * FOX_DIM), BF16),
        compiler_params=_params(2),
        name="fox_attention",
    )(qkv, qkv, qkv, dcum, dcum_t)


def _pad_cols(w, n):
    return jnp.pad(w, ((0, 0), (0, n - w.shape[1])))


def _swiglu_half(h, hn, w_in, w_out):
    n_main = (D_FF // FFN_TILE) * FFN_TILE
    act_main = ffn_in(hn, w_in, 0, n_main, FFN_TILE)
    act_tail = ffn_in(hn, w_in, n_main, D_FF - n_main, D_FF - n_main)
    return matmul_residual(h, w_out, [(act_main, 0), (act_tail, n_main)], scale=0.5)


def _ssd_hgrn_layer(h, hn, w_in, conv_w, conv_b, dt_bias, a_log, d_skip, ssd_norm_w, lb, hgrn_norm_w, w_out):
    d = D_MODEL
    nb = SSD_GROUPS * SSD_STATE
    o_dt = 2 * d + 2 * nb
    o_q = o_dt + SSD_HEADS
    tiles = 1024 // LANES
    proj = matmul(hn, w_in, BF16, o_dt, lambda j: j * tiles)
    dt_raw = matmul(hn, w_in, F32, LANES, lambda j: o_dt // LANES, tn=LANES)
    w_qfvg = w_in[:, o_q:]
    qvg = matmul(hn, w_qfvg, BF16, 3 * d, lambda j: (j + jnp.where(j >= 2, 2, 0)) * tiles)
    f_raw = matmul(hn, w_qfvg, F32, d, lambda j: (j + 2) * tiles)
    pad_heads = lambda v: jnp.pad(v, (0, LANES - SSD_HEADS)).reshape(1, LANES)
    y_a = ssd_mixer(proj, dt_raw, conv_w[:, :d], conv_b[:d].reshape(1, d), conv_w[:, d:],
                    conv_b[d:].reshape(1, 2 * nb), pad_heads(dt_bias), pad_heads(a_log),
                    jnp.repeat(d_skip, SSD_HEAD_DIM).reshape(1, d), ssd_norm_w.reshape(1, d))
    y_b = hgrn_mixer(qvg, f_raw, lb.reshape(1, d), hgrn_norm_w.reshape(1, d))
    return matmul_residual(h, w_out, [(y_a, 0), (y_b, d)], tm=1024)


def _fox_layer(h, hn, w_in, b_f, w_out):
    d = D_MODEL
    t = h.shape[0]
    tiles = 1024 // LANES
    qkv = matmul(hn, w_in, BF16, 3 * d, lambda j: j * tiles, scaled_blocks=d // 1024,
                 scale=LOG2E * FOX_DIM ** -0.5)
    f_raw = matmul(hn, _pad_cols(w_in[:, 3 * d:], LANES), F32, LANES, lambda j: 0, tn=LANES)
    b_pad = jnp.pad(b_f, (0, LANES - FOX_HEADS)).reshape(1, LANES)
    dcum, dcum_t = logf_cumsum(f_raw, b_pad)
    o = fox_attention(qkv, dcum, dcum_t[:FOX_HEADS].reshape(FOX_HEADS, 1, t))
    return matmul_residual(h, w_out, [(o, 0)], tm=1024)


def kernel(x, p, ffn1_norm, ffn1_w_in, ffn1_w_out, mix_norm, ab_w_in, ssd_conv_w, ssd_conv_b, ssd_dt_bias,
           ssd_a_log, ssd_d, ssd_norm, hgrn_lb_logits, hgrn_norm, ab_w_out, fox_w_in, fox_b_f, fox_w_out,
           ffn2_norm, ffn2_w_in, ffn2_w_out, ple_gate_norm, ple_w_gate, ple_w_up, ple_norm, final_norm):
    bsz, t, d = x.shape
    depth = p.shape[0]
    assert bsz == 1 and d == D_MODEL
    lb_all = jnp.cumsum(jax.nn.softmax(hgrn_lb_logits.astype(F32), axis=0), axis=0)
    h = x.reshape(t, d)
    hn = rmsnorm(h, ffn1_norm[0], BF16)
    for i in range(depth):
        j = i // 2
        h = _swiglu_half(h, hn, ffn1_w_in[i], ffn1_w_out[i])
        hn = rmsnorm(h, mix_norm[i], BF16)
        if i % 2 == 0:
            h = _ssd_hgrn_layer(h, hn, ab_w_in[j], ssd_conv_w[j], ssd_conv_b[j], ssd_dt_bias[j], ssd_a_log[j],
                                ssd_d[j], ssd_norm[j], lb_all[i], hgrn_norm[j], ab_w_out[j])
        else:
            h = _fox_layer(h, hn, fox_w_in[j], fox_b_f[j], fox_w_out[j])
        hn = rmsnorm(h, ffn2_norm[i], BF16)
        h = _swiglu_half(h, hn, ffn2_w_in[i], ffn2_w_out[i])
        hg = rmsnorm(h, ple_gate_norm[i], BF16)
        final = i == depth - 1
        next_w = final_norm if final else ffn1_norm[i + 1]
        outs = ple_add(h, hg, p[i].reshape(t, -1), ple_w_gate[i], ple_w_up[i], ple_norm[i], next_w, final)
        if final:
            return outs[0].reshape(bsz, t, d)
        h, hn = outs
```

```python
import functools
import math

import jax
import jax.numpy as jnp
from jax import lax
from jax.experimental import pallas as pl
from jax.experimental.pallas import tpu as pltpu

F32 = jnp.float32
BF16 = jnp.bfloat16
EPS = 1e-6

D_MODEL = 2048
D_FF = 5504
SSD_HEADS = 32
SSD_HEAD_DIM = 64
SSD_GROUPS = 4
SSD_STATE = 128
SSD_GROUP_WIDTH = 512
SSD_CONV = 4
HGRN_HEADS = 16
HGRN_DIM = 128
FOX_HEADS = 16
FOX_DIM = 128
LANES = 128
SUBLANES = 8
CONV_HALO = 8

SSD_ROWS = 128
HGRN_ROWS = 64
FOX_Q_BLOCK = 512
FOX_K_BLOCK = 1024
FOX_PACK = 2
FOX_F_LANE0 = LANES - FOX_HEADS
FFN_TILE = 512
VMEM_LIMIT = 56 * 1024 * 1024
LOG2E = math.log2(math.e)


def _params(n_axes, vmem=VMEM_LIMIT):
    return pltpu.CompilerParams(dimension_semantics=("arbitrary",) * n_axes, vmem_limit_bytes=vmem)


def _silu(x):
    return x * jax.nn.sigmoid(x)


def _softplus(x):
    return jnp.maximum(x, 0.0) + jnp.log1p(jnp.exp(-jnp.abs(x)))


def _dot(a, b):
    return jnp.dot(a, b, preferred_element_type=F32)


def _dot_nt(a, b):
    return lax.dot_general(a, b, (((1,), (1,)), ((), ())), preferred_element_type=F32)


def _dot_f32(a, b):
    return jnp.dot(a, b, preferred_element_type=F32, precision=lax.Precision.HIGHEST)


def _tril(n):
    r = lax.broadcasted_iota(jnp.int32, (n, n), 0)
    c = lax.broadcasted_iota(jnp.int32, (n, n), 1)
    return r >= c


def _weight_window(layer, rows, width, row_tile, col_tile):
    return pl.BlockSpec((pl.Element(1), pl.Element(rows), pl.Element(width)),
                        lambda j, i: (layer, row_tile(j) * SUBLANES, col_tile(j) * LANES))


def _cast_weights_once(pairs):
    @pl.when(pl.program_id(1) == 0)
    def _():
        for w_ref, wb_ref in pairs:
            wb_ref[...] = w_ref[0].astype(BF16)


def _rmsnorm_kernel(h_ref, w_ref, o_ref):
    x = h_ref[...]
    y = x * lax.rsqrt(jnp.mean(x * x, axis=-1, keepdims=True) + EPS)
    o_ref[...] = (y * w_ref[...]).astype(o_ref.dtype)


def rmsnorm(h, w, out_dtype, tm=512):
    m, d = h.shape
    return pl.pallas_call(
        _rmsnorm_kernel,
        grid=(m // tm,),
        in_specs=[pl.BlockSpec((tm, d), lambda i: (i, 0)), pl.BlockSpec((1, d), lambda i: (0, 0))],
        out_specs=pl.BlockSpec((tm, d), lambda i: (i, 0)),
        out_shape=jax.ShapeDtypeStruct((m, d), out_dtype),
        compiler_params=_params(1),
        name="rmsnorm",
    )(h, w.reshape(1, d))


def _mm_nt_kernel(x_ref, w_ref, o_ref, wb_ref, *, scaled_blocks, scale):
    _cast_weights_once([(w_ref, wb_ref)])
    acc = _dot_nt(x_ref[...], wb_ref[...])
    if scaled_blocks:
        acc = acc * jnp.where(pl.program_id(0) < scaled_blocks, scale, 1.0)
    o_ref[...] = acc.astype(o_ref.dtype)


def matmul_nt(x, wt, layer, out_dtype, n_out, row_tile, tn=1024, tm=1024, scaled_blocks=0, scale=1.0):
    m, k = x.shape
    return pl.pallas_call(
        functools.partial(_mm_nt_kernel, scaled_blocks=scaled_blocks, scale=scale),
        grid=(n_out // tn, m // tm),
        in_specs=[pl.BlockSpec((tm, k), lambda j, i: (i, 0)), _weight_window(layer, tn, k, row_tile, lambda j: 0)],
        out_specs=pl.BlockSpec((tm, tn), lambda j, i: (i, j)),
        out_shape=jax.ShapeDtypeStruct((m, n_out), out_dtype),
        scratch_shapes=[pltpu.VMEM((tn, k), BF16)],
        compiler_params=_params(2),
        name="matmul_nt",
    )(x, wt)


def _ffn_in_kernel(x_ref, wg_ref, wu_ref, o_ref, wgb_ref, wub_ref):
    _cast_weights_once([(wg_ref, wgb_ref), (wu_ref, wub_ref)])
    x = x_ref[...]
    gate = _dot(x, wgb_ref[...])
    up = _dot(x, wub_ref[...])
    o_ref[...] = (_silu(gate) * up).astype(o_ref.dtype)


def ffn_in(x, w_in, layer, col0, n_out, tn, tm=1024):
    m, k = x.shape
    blocks_per_tile = tn // LANES
    gate_col = lambda j: col0 // LANES + j * blocks_per_tile
    up_col = lambda j: (D_FF + col0) // LANES + j * blocks_per_tile
    return pl.pallas_call(
        _ffn_in_kernel,
        grid=(n_out // tn, m // tm),
        in_specs=[pl.BlockSpec((tm, k), lambda j, i: (i, 0)),
                  _weight_window(layer, k, tn, lambda j: 0, gate_col),
                  _weight_window(layer, k, tn, lambda j: 0, up_col)],
        out_specs=pl.BlockSpec((tm, tn), lambda j, i: (i, j)),
        out_shape=jax.ShapeDtypeStruct((m, n_out), BF16),
        scratch_shapes=[pltpu.VMEM((k, tn), BF16), pltpu.VMEM((k, tn), BF16)],
        compiler_params=_params(2),
        name="ffn_in",
    )(x, w_in, w_in)


def _mm_resid_kernel(*refs, n_pairs, scale):
    h_ref = refs[0]
    a_refs = refs[1:1 + n_pairs]
    w_refs = refs[1 + n_pairs:1 + 2 * n_pairs]
    o_ref = refs[1 + 2 * n_pairs]
    wb_refs = refs[2 + 2 * n_pairs:]
    _cast_weights_once(list(zip(w_refs, wb_refs)))
    acc = _dot(a_refs[0][...], wb_refs[0][...])
    for a_ref, wb_ref in zip(a_refs[1:], wb_refs[1:]):
        acc = acc + _dot(a_ref[...], wb_ref[...])
    if scale != 1.0:
        acc = scale * acc
    o_ref[...] = h_ref[...] + acc


def matmul_residual(h, w, layer, pieces, scale=1.0, tm=512, tn=512):
    m, n = h.shape
    blocks_per_tile = tn // LANES
    a_specs = [pl.BlockSpec((tm, a.shape[1]), lambda j, i: (i, 0)) for a, _ in pieces]
    w_specs = [_weight_window(layer, a.shape[1], tn, lambda j, r=row0 // SUBLANES: r, lambda j: j * blocks_per_tile)
               for a, row0 in pieces]
    return pl.pallas_call(
        functools.partial(_mm_resid_kernel, n_pairs=len(pieces), scale=scale),
        grid=(n // tn, m // tm),
        in_specs=[pl.BlockSpec((tm, tn), lambda j, i: (i, j))] + a_specs + w_specs,
        out_specs=pl.BlockSpec((tm, tn), lambda j, i: (i, j)),
        out_shape=jax.ShapeDtypeStruct((m, n), F32),
        scratch_shapes=[pltpu.VMEM((a.shape[1], tn), BF16) for a, _ in pieces],
        compiler_params=_params(2),
        name="matmul_residual",
    )(h, *[a for a, _ in pieces], *[w] * len(pieces))


def _ple_kernel(h_ref, hg_ref, p_ref, wg_ref, wu_ref, pn_ref, nn_ref, *refs, emit_h):
    wgb_ref, wub_ref = refs[-2:]
    out_refs = refs[:-2]

    @pl.when(pl.program_id(0) == 0)
    def _():
        wgb_ref[...] = wg_ref[...].astype(BF16)
        wub_ref[...] = wu_ref[...].astype(BF16)

    gate = jax.nn.sigmoid(_dot(hg_ref[...], wgb_ref[...]))
    e = _dot(p_ref[...].astype(BF16), wub_ref[...])
    emb = e * lax.rsqrt(jnp.mean(e * e, axis=-1, keepdims=True) + EPS) * pn_ref[...]
    h = h_ref[...] + emb * gate
    if emit_h:
        out_refs[0][...] = h
    n_ref = out_refs[-1]
    y = h * lax.rsqrt(jnp.mean(h * h, axis=-1, keepdims=True) + EPS)
    n_ref[...] = (y * nn_ref[...]).astype(n_ref.dtype)


def ple_add(h, hg, p, w_gate, w_up, layer, post_norm_w, next_norm_w, final, tm=256):
    m, d = h.shape
    pd = p.shape[-1]
    row = lambda i: (i, 0)
    fixed = lambda i: (0, 0)
    resident = functools.partial(pl.BlockSpec, index_map=lambda i: (layer, 0, 0), pipeline_mode=pl.Buffered(1))
    out_shape = [jax.ShapeDtypeStruct((m, d), F32 if final else BF16)]
    out_specs = [pl.BlockSpec((tm, d), row)]
    if not final:
        out_shape = [jax.ShapeDtypeStruct((m, d), F32)] + out_shape
        out_specs = [pl.BlockSpec((tm, d), row)] + out_specs
    return pl.pallas_call(
        functools.partial(_ple_kernel, emit_h=not final),
        grid=(m // tm,),
        in_specs=[pl.BlockSpec((tm, d), row), pl.BlockSpec((tm, d), row),
                  pl.BlockSpec((None, None, tm, pd), lambda i: (layer, 0, i, 0)),
                  resident((None, d, d)), resident((None, pd, d)),
                  pl.BlockSpec((1, d), fixed), pl.BlockSpec((1, d), fixed)],
        out_specs=out_specs,
        out_shape=out_shape,
        scratch_shapes=[pltpu.VMEM((d, d), BF16), pltpu.VMEM((pd, d), BF16)],
        compiler_params=_params(1),
        name="ple_add",
    )(h, hg, p, w_gate, w_up, post_norm_w.reshape(1, d), next_norm_w.reshape(1, d))


def _ssd_kernel(z_ref, xs_ref, bc_ref, dt_ref, cwx_ref, cbx_ref, cwbc_ref, cbbc_ref, dtb_ref, alog_ref,
                dskip_ref, nw_ref, o_ref, xpad_ref, bcpad_ref, st_ref):
    rows = SSD_ROWS

    @pl.when(pl.program_id(0) == 0)
    def _():
        xpad_ref[0:CONV_HALO, :] = jnp.zeros((CONV_HALO, xpad_ref.shape[1]), F32)
        bcpad_ref[0:CONV_HALO, :] = jnp.zeros((CONV_HALO, bcpad_ref.shape[1]), F32)
        st_ref[...] = jnp.zeros(st_ref.shape, F32)

    def conv_silu(pad_ref, raw_ref, w_ref, b_ref):
        pad_ref[CONV_HALO:CONV_HALO + rows, :] = raw_ref[...].astype(F32)
        acc = b_ref[...] + pad_ref[pl.ds(CONV_HALO - SSD_CONV + 1, rows), :] * w_ref[0:1, :]
        for k in range(1, SSD_CONV):
            acc = acc + pad_ref[pl.ds(CONV_HALO - SSD_CONV + 1 + k, rows), :] * w_ref[k:k + 1, :]
        pad_ref[0:CONV_HALO, :] = pad_ref[rows:rows + CONV_HALO, :]
        return _silu(acc)

    xs = conv_silu(xpad_ref, xs_ref, cwx_ref, cbx_ref)
    bc = conv_silu(bcpad_ref, bc_ref, cwbc_ref, cbbc_ref)

    dt = _softplus(dt_ref[...] + dtb_ref[...])
    a = -jnp.exp(alog_ref[...])
    causal = _tril(rows)
    cum = _dot_f32(causal.astype(F32), dt * a)
    cum_t = cum.T
    low_half = lax.broadcasted_iota(jnp.int32, (rows, LANES), 1) < SSD_HEAD_DIM

    def per_pair(v, h0):
        return jnp.where(low_half, v[:, h0:h0 + 1], v[:, h0 + 1:h0 + 2])

    pairs_per_group = SSD_GROUP_WIDTH // LANES
    for g in range(SSD_GROUPS):
        b_g = bc[:, g * SSD_STATE:(g + 1) * SSD_STATE]
        c_g = bc[:, (SSD_GROUPS + g) * SSD_STATE:(SSD_GROUPS + g + 1) * SSD_STATE].astype(BF16)
        cb = _dot_nt(c_g, b_g.astype(BF16))
        st = st_ref[g]
        y_off = _dot(c_g, st.astype(BF16))
        y_tiles, xw_tiles, dec_tiles = [], [], []
        for jj in range(pairs_per_group):
            j = g * pairs_per_group + jj
            h0 = 2 * j
            sl = slice(j * LANES, (j + 1) * LANES)
            x_p = xs[:, sl]
            cum_p = per_pair(cum, h0)
            xdt = x_p * per_pair(dt, h0)
            xdt_b = xdt.astype(BF16)
            halves = []
            for h in (h0, h0 + 1):
                diff = cum[:, h:h + 1] - cum_t[h:h + 1, :]
                decay = jnp.exp(jnp.where(causal, diff, -jnp.inf))
                halves.append(_dot((cb * decay).astype(BF16), xdt_b))
            y_diag = jnp.where(low_half, halves[0], halves[1])
            cum_last = cum_p[rows - 1:rows, :]
            xw_tiles.append((xdt * jnp.exp(cum_last - cum_p)).astype(BF16))
            dec_tiles.append(jnp.exp(cum_last))
            y = y_diag + y_off[:, jj * LANES:(jj + 1) * LANES] * jnp.exp(cum_p) + x_p * dskip_ref[:, sl]
            y_tiles.append(y * _silu(z_ref[:, sl].astype(F32)))
        xw = jnp.concatenate(xw_tiles, axis=1)
        st_ref[g] = st * jnp.concatenate(dec_tiles, axis=1) + _dot(b_g.T.astype(BF16), xw)
        ss = jnp.sum(y_tiles[0] * y_tiles[0], axis=-1, keepdims=True)
        for t in y_tiles[1:]:
            ss = ss + jnp.sum(t * t, axis=-1, keepdims=True)
        inv = lax.rsqrt(ss * (1.0 / SSD_GROUP_WIDTH) + EPS)
        for jj in range(pairs_per_group):
            sl = slice((g * pairs_per_group + jj) * LANES, (g * pairs_per_group + jj + 1) * LANES)
            o_ref[:, sl] = (y_tiles[jj] * inv * nw_ref[:, sl]).astype(o_ref.dtype)


def ssd_mixer(proj, dt_raw, conv_wx, conv_bx, conv_wbc, conv_bbc, dt_bias, a_log, d_skip_x, norm_w):
    t = proj.shape[0]
    rows = SSD_ROWS
    d = D_MODEL
    nbc = 2 * SSD_GROUPS * SSD_STATE
    fixed = lambda i: (0, 0)
    return pl.pallas_call(
        _ssd_kernel,
        grid=(t // rows,),
        in_specs=[pl.BlockSpec((rows, d), lambda i: (i, 0)),
                  pl.BlockSpec((rows, d), lambda i: (i, 1)),
                  pl.BlockSpec((rows, nbc), lambda i: (i, 2 * d // nbc)),
                  pl.BlockSpec((rows, LANES), lambda i: (i, 0)),
                  pl.BlockSpec((SSD_CONV, d), fixed), pl.BlockSpec((1, d), fixed),
                  pl.BlockSpec((SSD_CONV, nbc), fixed), pl.BlockSpec((1, nbc), fixed),
                  pl.BlockSpec((1, LANES), fixed), pl.BlockSpec((1, LANES), fixed),
                  pl.BlockSpec((1, d), fixed), pl.BlockSpec((1, d), fixed)],
        out_specs=pl.BlockSpec((rows, d), lambda i: (i, 0)),
        out_shape=jax.ShapeDtypeStruct((t, d), BF16),
        scratch_shapes=[pltpu.VMEM((rows + CONV_HALO, d), F32),
                        pltpu.VMEM((rows + CONV_HALO, nbc), F32),
                        pltpu.VMEM((SSD_GROUPS, SSD_STATE, SSD_GROUP_WIDTH), F32)],
        compiler_params=_params(1),
        name="ssd_mixer",
    )(proj, proj, proj, dt_raw, conv_wx, conv_bx, conv_wbc, conv_bbc, dt_bias, a_log, d_skip_x, norm_w)


def _hgrn_kernel(q_ref, v_ref, g_ref, f_ref, lb_ref, nw_ref, o_ref, st_ref):
    rows = HGRN_ROWS

    @pl.when(pl.program_id(0) == 0)
    def _():
        st_ref[...] = jnp.zeros(st_ref.shape, F32)

    lb = lb_ref[...]
    f = lb + (1.0 - lb) * jax.nn.sigmoid(f_ref[...])
    k = 1.0 - f
    causal = _tril(rows)
    cum = _dot_f32(causal.astype(F32), jnp.log(f))
    qf = _silu(q_ref[...].astype(F32))
    mid = cum[rows // 2 - 1:rows // 2, :]
    last = cum[rows - 1:rows, :]
    q_rel = (qf * jnp.exp(cum - mid)).astype(BF16)
    k_rel = (k * jnp.exp(mid - cum)).astype(BF16)
    k_end = (k * jnp.exp(last - cum)).astype(BF16)
    q_dec = (qf * jnp.exp(cum)).astype(BF16)
    chunk_decay = jnp.exp(last)
    for h in range(HGRN_HEADS):
        sl = slice(h * HGRN_DIM, (h + 1) * HGRN_DIM)
        v_h = v_ref[:, sl]
        att = jnp.where(causal, _dot_nt(q_rel[:, sl], k_rel[:, sl]), 0.0)
        st = st_ref[h]
        o = _dot(att.astype(BF16), v_h) + _dot_nt(q_dec[:, sl], st.astype(BF16))
        st_ref[h] = st * chunk_decay[:, sl] + _dot(v_h.astype(F32).T.astype(BF16), k_end[:, sl])
        on = o * lax.rsqrt(jnp.mean(o * o, axis=-1, keepdims=True) + EPS) * nw_ref[:, sl]
        o_ref[:, sl] = (on * _silu(g_ref[:, sl].astype(F32))).astype(o_ref.dtype)


def hgrn_mixer(qvg, f_raw, lb, norm_w):
    t = qvg.shape[0]
    rows = HGRN_ROWS
    d = D_MODEL
    fixed = lambda i: (0, 0)
    return pl.pallas_call(
        _hgrn_kernel,
        grid=(t // rows,),
        in_specs=[pl.BlockSpec((rows, d), lambda i: (i, 0)),
                  pl.BlockSpec((rows, d), lambda i: (i, 1)),
                  pl.BlockSpec((rows, d), lambda i: (i, 2)),
                  pl.BlockSpec((rows, d), lambda i: (i, 0)),
                  pl.BlockSpec((1, d), fixed), pl.BlockSpec((1, d), fixed)],
        out_specs=pl.BlockSpec((rows, d), lambda i: (i, 0)),
        out_shape=jax.ShapeDtypeStruct((t, d), BF16),
        scratch_shapes=[pltpu.VMEM((HGRN_HEADS, HGRN_DIM, HGRN_DIM), F32)],
        compiler_params=_params(1),
        name="hgrn_mixer",
    )(qvg, qvg, qvg, f_raw, lb, norm_w)


def _logf_cumsum_kernel(f_ref, b_ref, o_ref, ot_ref, carry_ref):
    rows = f_ref.shape[0]

    @pl.when(pl.program_id(0) == 0)
    def _():
        carry_ref[...] = jnp.zeros(carry_ref.shape, F32)

    log_f = -_softplus(-(f_ref[...] + b_ref[...]))
    c = _dot_f32(_tril(rows).astype(F32), log_f) + carry_ref[...]
    carry_ref[...] = c[rows - 1:rows, :]
    c2 = c * LOG2E
    o_ref[...] = c2
    ot_ref[...] = c2.T


def logf_cumsum(f_raw, b_f, rows=512):
    t = f_raw.shape[0]
    return pl.pallas_call(
        _logf_cumsum_kernel,
        grid=(t // rows,),
        in_specs=[pl.BlockSpec((rows, LANES), lambda i: (i, 0)), pl.BlockSpec((1, LANES), lambda i: (0, 0))],
        out_specs=[pl.BlockSpec((rows, LANES), lambda i: (i, 0)), pl.BlockSpec((LANES, rows), lambda i: (0, i))],
        out_shape=[jax.ShapeDtypeStruct((t, LANES), F32), jax.ShapeDtypeStruct((LANES, t), F32)],
        scratch_shapes=[pltpu.VMEM((1, LANES), F32)],
        compiler_params=_params(1),
        name="logf_cumsum",
    )(f_raw, b_f)


def _fox_kernel(q_ref, k_ref, v_ref, dq_ref, dk_ref, o_ref):
    tq, tk = FOX_Q_BLOCK, FOX_K_BLOCK
    hp = pl.program_id(0)
    qi = pl.program_id(1)
    lane = lax.broadcasted_iota(jnp.int32, (tq, LANES), 1)
    dq_all = dq_ref[...]
    heads = []
    for hh in range(FOX_PACK):
        sl = slice(hh * FOX_DIM, (hh + 1) * FOX_DIM)
        dq = jnp.sum(jnp.where(lane == FOX_F_LANE0 + hp * FOX_PACK + hh, dq_all, 0.0), axis=-1, keepdims=True)
        heads.append((hh, sl, q_ref[:, sl], dq))
    causal = _tril(tq)

    def step(start, width, carry, diagonal):
        out = []
        for (hh, sl, q, dq), (m, l, acc) in zip(heads, carry):
            k_b = k_ref[pl.ds(start, width), sl]
            v_b = v_ref[pl.ds(start, width), sl]
            s = _dot_nt(q, k_b) - dk_ref[hh, :, pl.ds(start, width)]
            if diagonal:
                s = jnp.where(causal, s, -jnp.inf)
            m_new = jnp.maximum(m, jnp.max(s, axis=-1, keepdims=True) + dq)
            alpha = jnp.exp2(m - m_new)
            p = jnp.exp2(s - (m_new - dq))
            l = alpha * l + jnp.sum(p, axis=-1, keepdims=True)
            acc = alpha * acc + _dot(p.astype(BF16), v_b)
            out.append((m_new, l, acc))
        return tuple(out)

    init = tuple((jnp.full((tq, 1), -jnp.inf, F32), jnp.zeros((tq, 1), F32), jnp.zeros((tq, FOX_DIM), F32))
                 for _ in heads)
    n_wide = lax.shift_right_logical(qi, 1)
    carry = lax.fori_loop(0, n_wide, lambda ki, c: step(pl.multiple_of(ki * tk, tk), tk, c, False), init)
    odd_start = pl.multiple_of(n_wide * tk, tk)
    carry = lax.fori_loop(0, qi & 1, lambda _, c: step(odd_start, tq, c, False), carry)
    carry = step(pl.multiple_of(qi * tq, tq), tq, carry, True)
    for (hh, sl, _, _), (_, l, acc) in zip(heads, carry):
        o_ref[:, sl] = (acc / l).astype(o_ref.dtype)


def fox_attention(qkv, dcum, dcum_t):
    t = qkv.shape[0]
    tq = FOX_Q_BLOCK
    width = FOX_PACK * FOX_DIM
    groups = FOX_HEADS // FOX_PACK
    return pl.pallas_call(
        _fox_kernel,
        grid=(groups, t // tq),
        in_specs=[pl.BlockSpec((tq, width), lambda h, i: (i, h)),
                  pl.BlockSpec((t, width), lambda h, i: (0, groups + h)),
                  pl.BlockSpec((t, width), lambda h, i: (0, 2 * groups + h)),
                  pl.BlockSpec((tq, LANES), lambda h, i: (i, 0)),
                  pl.BlockSpec((FOX_PACK, 1, t), lambda h, i: (h, 0, 0))],
        out_specs=pl.BlockSpec((tq, width), lambda h, i: (i, h)),
        out_shape=jax.ShapeDtypeStruct((t, FOX_HEADS * FOX_DIM), BF16),
        compiler_params=_params(2),
        name="fox_attention",
    )(qkv, qkv, qkv, dcum, dcum_t)


def _swiglu_half(h, hn, w_in, w_out, layer):
    n_main = (D_FF // FFN_TILE) * FFN_TILE
    act_main = ffn_in(hn, w_in, layer, 0, n_main, FFN_TILE)
    act_tail = ffn_in(hn, w_in, layer, n_main, D_FF - n_main, D_FF - n_main)
    return matmul_residual(h, w_out, layer, [(act_main, 0), (act_tail, n_main)], scale=0.5)


def _ssd_hgrn_layer(h, hn, w_in, layer, conv_w, conv_b, dt_bias, a_log, d_skip, ssd_norm_w, lb, hgrn_norm_w, w_out):
    d = D_MODEL
    nb = SSD_GROUPS * SSD_STATE
    o_dt = 2 * d + 2 * nb
    o_q = o_dt + SSD_HEADS
    tile = 1024 // SUBLANES
    wt = jnp.swapaxes(w_in, 1, 2)
    proj = matmul_nt(hn, wt, layer, BF16, o_dt, lambda j: j * tile)
    dt_raw = matmul_nt(hn, wt, layer, F32, LANES, lambda j: o_dt // SUBLANES, tn=LANES)
    first = o_q // SUBLANES
    qvg = matmul_nt(hn, wt, layer, BF16, 3 * d, lambda j: first + (j + jnp.where(j >= 2, 2, 0)) * tile)
    f_raw = matmul_nt(hn, wt, layer, F32, d, lambda j: first + (j + 2) * tile)
    pad_heads = lambda v: jnp.pad(v, (0, LANES - SSD_HEADS)).reshape(1, LANES)
    y_a = ssd_mixer(proj, dt_raw, conv_w[:, :d], conv_b[:d].reshape(1, d), conv_w[:, d:],
                    conv_b[d:].reshape(1, 2 * nb), pad_heads(dt_bias), pad_heads(a_log),
                    jnp.repeat(d_skip, SSD_HEAD_DIM).reshape(1, d), ssd_norm_w.reshape(1, d))
    y_b = hgrn_mixer(qvg, f_raw, lb.reshape(1, d), hgrn_norm_w.reshape(1, d))
    return matmul_residual(h, w_out, layer, [(y_a, 0), (y_b, d)], tm=1024)


def _fox_layer(h, hn, w_in, layer, b_f, w_out):
    d = D_MODEL
    t = h.shape[0]
    tile = 1024 // SUBLANES
    wt = jnp.swapaxes(w_in, 1, 2)
    qkv = matmul_nt(hn, wt, layer, BF16, 3 * d, lambda j: j * tile, scaled_blocks=d // 1024,
                    scale=LOG2E * FOX_DIM ** -0.5)
    f_raw = matmul_nt(hn, wt, layer, F32, LANES, lambda j: (3 * d + FOX_HEADS - LANES) // SUBLANES, tn=LANES)
    b_pad = jnp.pad(b_f, (FOX_F_LANE0, 0)).reshape(1, LANES)
    dcum, dcum_t = logf_cumsum(f_raw, b_pad)
    o = fox_attention(qkv, dcum, dcum_t[FOX_F_LANE0:].reshape(FOX_HEADS, 1, t))
    return matmul_residual(h, w_out, layer, [(o, 0)], tm=1024)


def kernel(x, p, ffn1_norm, ffn1_w_in, ffn1_w_out, mix_norm, ab_w_in, ssd_conv_w, ssd_conv_b, ssd_dt_bias,
           ssd_a_log, ssd_d, ssd_norm, hgrn_lb_logits, hgrn_norm, ab_w_out, fox_w_in, fox_b_f, fox_w_out,
           ffn2_norm, ffn2_w_in, ffn2_w_out, ple_gate_norm, ple_w_gate, ple_w_up, ple_norm, final_norm):
    bsz, t, d = x.shape
    depth = p.shape[0]
    assert bsz == 1 and d == D_MODEL
    lb_all = jnp.cumsum(jax.nn.softmax(hgrn_lb_logits.astype(F32), axis=0), axis=0)
    h = x.reshape(t, d)
    hn = rmsnorm(h, ffn1_norm[0], BF16)
    for i in range(depth):
        j = i // 2
        h = _swiglu_half(h, hn, ffn1_w_in, ffn1_w_out, i)
        hn = rmsnorm(h, mix_norm[i], BF16)
        if i % 2 == 0:
            h = _ssd_hgrn_layer(h, hn, ab_w_in, j, ssd_conv_w[j], ssd_conv_b[j], ssd_dt_bias[j], ssd_a_log[j],
                                ssd_d[j], ssd_norm[j], lb_all[i], hgrn_norm[j], ab_w_out)
        else:
            h = _fox_layer(h, hn, fox_w_in, j, fox_b_f[j], fox_w_out)
        hn = rmsnorm(h, ffn2_norm[i], BF16)
        h = _swiglu_half(h, hn, ffn2_w_in, ffn2_w_out, i)
        hg = rmsnorm(h, ple_gate_norm[i], BF16)
        final = i == depth - 1
        next_w = final_norm if final else ffn1_norm[i + 1]
        outs = ple_add(h, hg, p, ple_w_gate, ple_w_up, i, ple_norm[i], next_w, final)
        if final:
            return outs[0].reshape(bsz, t, d)
        h, hn = outs
```

```python
import functools
import math

import jax
import jax.numpy as jnp
from jax import lax
from jax.experimental import pallas as pl
from jax.experimental.pallas import tpu as pltpu

F32 = jnp.float32
BF16 = jnp.bfloat16
EPS = 1e-6

D_MODEL = 2048
D_FF = 5504
SSD_HEADS = 32
SSD_HEAD_DIM = 64
SSD_GROUPS = 4
SSD_STATE = 128
SSD_GROUP_WIDTH = 512
SSD_CONV = 4
HGRN_HEADS = 16
HGRN_DIM = 128
FOX_HEADS = 16
FOX_DIM = 128
LANES = 128
SUBLANES = 8
CONV_HALO = 8

SSD_ROWS = 128
HGRN_ROWS = 64
FOX_Q_BLOCK = 1024
FOX_K_BLOCK = 2048
FOX_F_LANE0 = LANES - FOX_HEADS
FFN_TILE = 512
VMEM_LIMIT = 56 * 1024 * 1024
LOG2E = math.log2(math.e)


def _params(n_axes, vmem=VMEM_LIMIT):
    return pltpu.CompilerParams(dimension_semantics=("arbitrary",) * n_axes, vmem_limit_bytes=vmem)


def _silu(x):
    return x * jax.nn.sigmoid(x)


def _softplus(x):
    return jnp.maximum(x, 0.0) + jnp.log(1.0 + jnp.exp(-jnp.abs(x)))


def _dot(a, b):
    return jnp.dot(a, b, preferred_element_type=F32)


def _dot_nt(a, b):
    return lax.dot_general(a, b, (((1,), (1,)), ((), ())), preferred_element_type=F32)


def _dot_tn(a, b):
    return lax.dot_general(a, b, (((0,), (0,)), ((), ())), preferred_element_type=F32)


def _dot_f32(a, b):
    return jnp.dot(a, b, preferred_element_type=F32, precision=lax.Precision.HIGHEST)


def _tril(n):
    r = lax.broadcasted_iota(jnp.int32, (n, n), 0)
    c = lax.broadcasted_iota(jnp.int32, (n, n), 1)
    return r >= c


def _weight_window(layer, rows, width, row_tile, col_tile):
    return pl.BlockSpec((pl.Element(1), pl.Element(rows), pl.Element(width)),
                        lambda j, i: (layer, row_tile(j) * SUBLANES, col_tile(j) * LANES))


def _cast_weights_once(pairs):
    @pl.when(pl.program_id(1) == 0)
    def _():
        for w_ref, wb_ref in pairs:
            wb_ref[...] = w_ref[0].astype(BF16)


def _rmsnorm_kernel(h_ref, w_ref, o_ref):
    x = h_ref[...]
    y = x * lax.rsqrt(jnp.mean(x * x, axis=-1, keepdims=True) + EPS)
    o_ref[...] = (y * w_ref[...]).astype(o_ref.dtype)


def rmsnorm(h, w, out_dtype, tm=512):
    m, d = h.shape
    return pl.pallas_call(
        _rmsnorm_kernel,
        grid=(m // tm,),
        in_specs=[pl.BlockSpec((tm, d), lambda i: (i, 0)), pl.BlockSpec((1, d), lambda i: (0, 0))],
        out_specs=pl.BlockSpec((tm, d), lambda i: (i, 0)),
        out_shape=jax.ShapeDtypeStruct((m, d), out_dtype),
        compiler_params=_params(1),
        name="rmsnorm",
    )(h, w.reshape(1, d))


def _mm_nt_kernel(x_ref, w_ref, o_ref, wb_ref, *, scaled_blocks, scale):
    _cast_weights_once([(w_ref, wb_ref)])
    acc = _dot_nt(x_ref[...], wb_ref[...])
    if scaled_blocks:
        acc = acc * jnp.where(pl.program_id(0) < scaled_blocks, scale, 1.0)
    o_ref[...] = acc.astype(o_ref.dtype)


def matmul_nt(x, wt, layer, out_dtype, n_out, row_tile, tn=1024, tm=1024, scaled_blocks=0, scale=1.0):
    m, k = x.shape
    return pl.pallas_call(
        functools.partial(_mm_nt_kernel, scaled_blocks=scaled_blocks, scale=scale),
        grid=(n_out // tn, m // tm),
        in_specs=[pl.BlockSpec((tm, k), lambda j, i: (i, 0)), _weight_window(layer, tn, k, row_tile, lambda j: 0)],
        out_specs=pl.BlockSpec((tm, tn), lambda j, i: (i, j)),
        out_shape=jax.ShapeDtypeStruct((m, n_out), out_dtype),
        scratch_shapes=[pltpu.VMEM((tn, k), BF16)],
        compiler_params=_params(2),
        name="matmul_nt",
    )(x, wt)


def _ffn_in_kernel(x_ref, wg_ref, wu_ref, o_ref, wgb_ref, wub_ref):
    _cast_weights_once([(wg_ref, wgb_ref), (wu_ref, wub_ref)])
    x = x_ref[...]
    gate = _dot(x, wgb_ref[...])
    up = _dot(x, wub_ref[...])
    o_ref[...] = (_silu(gate) * up).astype(o_ref.dtype)


def ffn_in(x, w_in, layer, col0, n_out, tn, tm=1024):
    m, k = x.shape
    blocks_per_tile = tn // LANES
    gate_col = lambda j: col0 // LANES + j * blocks_per_tile
    up_col = lambda j: (D_FF + col0) // LANES + j * blocks_per_tile
    return pl.pallas_call(
        _ffn_in_kernel,
        grid=(n_out // tn, m // tm),
        in_specs=[pl.BlockSpec((tm, k), lambda j, i: (i, 0)),
                  _weight_window(layer, k, tn, lambda j: 0, gate_col),
                  _weight_window(layer, k, tn, lambda j: 0, up_col)],
        out_specs=pl.BlockSpec((tm, tn), lambda j, i: (i, j)),
        out_shape=jax.ShapeDtypeStruct((m, n_out), BF16),
        scratch_shapes=[pltpu.VMEM((k, tn), BF16), pltpu.VMEM((k, tn), BF16)],
        compiler_params=_params(2),
        name="ffn_in",
    )(x, w_in, w_in)


def _mm_resid_kernel(*refs, n_pairs, scale):
    h_ref = refs[0]
    a_refs = refs[1:1 + n_pairs]
    w_refs = refs[1 + n_pairs:1 + 2 * n_pairs]
    o_ref = refs[1 + 2 * n_pairs]
    wb_refs = refs[2 + 2 * n_pairs:]
    _cast_weights_once(list(zip(w_refs, wb_refs)))
    acc = _dot(a_refs[0][...], wb_refs[0][...])
    for a_ref, wb_ref in zip(a_refs[1:], wb_refs[1:]):
        acc = acc + _dot(a_ref[...], wb_ref[...])
    if scale != 1.0:
        acc = scale * acc
    o_ref[...] = h_ref[...] + acc


def matmul_residual(h, w, layer, pieces, scale=1.0, tm=512, tn=512):
    m, n = h.shape
    blocks_per_tile = tn // LANES
    a_specs = [pl.BlockSpec((tm, a.shape[1]), lambda j, i: (i, 0)) for a, _ in pieces]
    w_specs = [_weight_window(layer, a.shape[1], tn, lambda j, r=row0 // SUBLANES: r, lambda j: j * blocks_per_tile)
               for a, row0 in pieces]
    return pl.pallas_call(
        functools.partial(_mm_resid_kernel, n_pairs=len(pieces), scale=scale),
        grid=(n // tn, m // tm),
        in_specs=[pl.BlockSpec((tm, tn), lambda j, i: (i, j))] + a_specs + w_specs,
        out_specs=pl.BlockSpec((tm, tn), lambda j, i: (i, j)),
        out_shape=jax.ShapeDtypeStruct((m, n), F32),
        scratch_shapes=[pltpu.VMEM((a.shape[1], tn), BF16) for a, _ in pieces],
        compiler_params=_params(2),
        name="matmul_residual",
    )(h, *[a for a, _ in pieces], *[w] * len(pieces))


def _ple_kernel(h_ref, p_ref, wg_ref, wu_ref, gn_ref, pn_ref, nn_ref, *refs, final):
    wgb_ref, wub_ref = refs[-2:]
    out_refs = refs[:-2]

    @pl.when(pl.program_id(0) == 0)
    def _():
        wgb_ref[...] = wg_ref[...].astype(BF16)
        wub_ref[...] = wu_ref[...].astype(BF16)

    def normed(x, w_ref):
        return x * lax.rsqrt(jnp.mean(x * x, axis=-1, keepdims=True) + EPS) * w_ref[...]

    h = h_ref[...]
    gate = jax.nn.sigmoid(_dot(normed(h, gn_ref).astype(BF16), wgb_ref[...]))
    emb = normed(_dot(p_ref[...].astype(BF16), wub_ref[...]), pn_ref)
    h = h + emb * gate
    if not final:
        out_refs[0][...] = h
    out_refs[-1][...] = normed(h, nn_ref).astype(out_refs[-1].dtype)


def ple_add(h, p, w_gate, w_up, layer, gate_norm_w, post_norm_w, next_norm_w, final, tm=256):
    m, d = h.shape
    pd = p.shape[-1]
    row = lambda i: (i, 0)
    fixed = lambda i: (0, 0)
    resident = functools.partial(pl.BlockSpec, index_map=lambda i: (layer, 0, 0), pipeline_mode=pl.Buffered(1))
    out_dtypes = [F32] if final else [F32, BF16]
    return pl.pallas_call(
        functools.partial(_ple_kernel, final=final),
        grid=(m // tm,),
        in_specs=[pl.BlockSpec((tm, d), row),
                  pl.BlockSpec((None, None, tm, pd), lambda i: (layer, 0, i, 0)),
                  resident((None, d, d)), resident((None, pd, d)),
                  pl.BlockSpec((1, d), fixed), pl.BlockSpec((1, d), fixed), pl.BlockSpec((1, d), fixed)],
        out_specs=[pl.BlockSpec((tm, d), row) for _ in out_dtypes],
        out_shape=[jax.ShapeDtypeStruct((m, d), dt) for dt in out_dtypes],
        scratch_shapes=[pltpu.VMEM((d, d), BF16), pltpu.VMEM((pd, d), BF16)],
        compiler_params=_params(1),
        name="ple_add",
    )(h, p, w_gate, w_up, gate_norm_w.reshape(1, d), post_norm_w.reshape(1, d), next_norm_w.reshape(1, d))


def _ssd_kernel(z_ref, xs_ref, bc_ref, dt_ref, cwx_ref, cbx_ref, cwbc_ref, cbbc_ref, dtb_ref, alog_ref,
                dskip_ref, nw_ref, o_ref, xpad_ref, bcpad_ref, st_ref):
    rows = SSD_ROWS

    @pl.when(pl.program_id(0) == 0)
    def _():
        xpad_ref[0:CONV_HALO, :] = jnp.zeros((CONV_HALO, xpad_ref.shape[1]), F32)
        bcpad_ref[0:CONV_HALO, :] = jnp.zeros((CONV_HALO, bcpad_ref.shape[1]), F32)
        st_ref[...] = jnp.zeros(st_ref.shape, F32)

    def conv_silu(pad_ref, raw_ref, w_ref, b_ref):
        pad_ref[CONV_HALO:CONV_HALO + rows, :] = raw_ref[...].astype(F32)
        acc = b_ref[...] + pad_ref[pl.ds(CONV_HALO - SSD_CONV + 1, rows), :] * w_ref[0:1, :]
        for k in range(1, SSD_CONV):
            acc = acc + pad_ref[pl.ds(CONV_HALO - SSD_CONV + 1 + k, rows), :] * w_ref[k:k + 1, :]
        pad_ref[0:CONV_HALO, :] = pad_ref[rows:rows + CONV_HALO, :]
        return _silu(acc)

    xs = conv_silu(xpad_ref, xs_ref, cwx_ref, cbx_ref)
    bc = conv_silu(bcpad_ref, bc_ref, cwbc_ref, cbbc_ref)

    dt = _softplus(dt_ref[...] + dtb_ref[...])
    a = -jnp.exp(alog_ref[...])
    causal = _tril(rows)
    cum = _dot_f32(causal.astype(F32), dt * a)
    cum_t = cum.T
    low_half = lax.broadcasted_iota(jnp.int32, (rows, LANES), 1) < SSD_HEAD_DIM

    def per_pair(v, h0):
        return jnp.where(low_half, v[:, h0:h0 + 1], v[:, h0 + 1:h0 + 2])

    pairs_per_group = SSD_GROUP_WIDTH // LANES
    for g in range(SSD_GROUPS):
        b_g = bc[:, g * SSD_STATE:(g + 1) * SSD_STATE]
        c_g = bc[:, (SSD_GROUPS + g) * SSD_STATE:(SSD_GROUPS + g + 1) * SSD_STATE].astype(BF16)
        cb = _dot_nt(c_g, b_g.astype(BF16))
        st = st_ref[g]
        y_off = _dot(c_g, st.astype(BF16))
        y_tiles, xw_tiles, dec_tiles = [], [], []
        for jj in range(pairs_per_group):
            j = g * pairs_per_group + jj
            h0 = 2 * j
            sl = slice(j * LANES, (j + 1) * LANES)
            x_p = xs[:, sl]
            cum_p = per_pair(cum, h0)
            xdt = x_p * per_pair(dt, h0)
            xdt_b = xdt.astype(BF16)
            halves = []
            for h in (h0, h0 + 1):
                diff = cum[:, h:h + 1] - cum_t[h:h + 1, :]
                decay = jnp.exp(jnp.where(causal, diff, -jnp.inf))
                halves.append(_dot((cb * decay).astype(BF16), xdt_b))
            y_diag = jnp.where(low_half, halves[0], halves[1])
            cum_last = cum_p[rows - 1:rows, :]
            xw_tiles.append((xdt * jnp.exp(cum_last - cum_p)).astype(BF16))
            dec_tiles.append(jnp.exp(cum_last))
            y = y_diag + y_off[:, jj * LANES:(jj + 1) * LANES] * jnp.exp(cum_p) + x_p * dskip_ref[:, sl]
            y_tiles.append(y * _silu(z_ref[:, sl].astype(F32)))
        xw = jnp.concatenate(xw_tiles, axis=1)
        st_ref[g] = st * jnp.concatenate(dec_tiles, axis=1) + _dot(b_g.T.astype(BF16), xw)
        ss = jnp.sum(y_tiles[0] * y_tiles[0], axis=-1, keepdims=True)
        for t in y_tiles[1:]:
            ss = ss + jnp.sum(t * t, axis=-1, keepdims=True)
        inv = lax.rsqrt(ss * (1.0 / SSD_GROUP_WIDTH) + EPS)
        for jj in range(pairs_per_group):
            sl = slice((g * pairs_per_group + jj) * LANES, (g * pairs_per_group + jj + 1) * LANES)
            o_ref[:, sl] = (y_tiles[jj] * inv * nw_ref[:, sl]).astype(o_ref.dtype)


def ssd_mixer(proj, dt_raw, conv_wx, conv_bx, conv_wbc, conv_bbc, dt_bias, a_log, d_skip_x, norm_w):
    t = proj.shape[0]
    rows = SSD_ROWS
    d = D_MODEL
    nbc = 2 * SSD_GROUPS * SSD_STATE
    fixed = lambda i: (0, 0)
    return pl.pallas_call(
        _ssd_kernel,
        grid=(t // rows,),
        in_specs=[pl.BlockSpec((rows, d), lambda i: (i, 0)),
                  pl.BlockSpec((rows, d), lambda i: (i, 1)),
                  pl.BlockSpec((rows, nbc), lambda i: (i, 2 * d // nbc)),
                  pl.BlockSpec((rows, LANES), lambda i: (i, 0)),
                  pl.BlockSpec((SSD_CONV, d), fixed), pl.BlockSpec((1, d), fixed),
                  pl.BlockSpec((SSD_CONV, nbc), fixed), pl.BlockSpec((1, nbc), fixed),
                  pl.BlockSpec((1, LANES), fixed), pl.BlockSpec((1, LANES), fixed),
                  pl.BlockSpec((1, d), fixed), pl.BlockSpec((1, d), fixed)],
        out_specs=pl.BlockSpec((rows, d), lambda i: (i, 0)),
        out_shape=jax.ShapeDtypeStruct((t, d), BF16),
        scratch_shapes=[pltpu.VMEM((rows + CONV_HALO, d), F32),
                        pltpu.VMEM((rows + CONV_HALO, nbc), F32),
                        pltpu.VMEM((SSD_GROUPS, SSD_STATE, SSD_GROUP_WIDTH), F32)],
        compiler_params=_params(1),
        name="ssd_mixer",
    )(proj, proj, proj, dt_raw, conv_wx, conv_bx, conv_wbc, conv_bbc, dt_bias, a_log, d_skip_x, norm_w)


def _hgrn_kernel(q_ref, v_ref, g_ref, f_ref, lb_ref, nw_ref, o_ref, st_ref):
    rows = HGRN_ROWS

    @pl.when(pl.program_id(0) == 0)
    def _():
        st_ref[...] = jnp.zeros(st_ref.shape, F32)

    lb = lb_ref[...]
    f = lb + (1.0 - lb) * jax.nn.sigmoid(f_ref[...])
    k = 1.0 - f
    causal = _tril(rows)
    cum = _dot_f32(causal.astype(F32), jnp.log(f))
    qf = _silu(q_ref[...].astype(F32))
    mid = cum[rows // 2 - 1:rows // 2, :]
    last = cum[rows - 1:rows, :]
    q_rel = (qf * jnp.exp(cum - mid)).astype(BF16)
    k_rel = (k * jnp.exp(mid - cum)).astype(BF16)
    k_end = (k * jnp.exp(last - cum)).astype(BF16)
    q_dec = (qf * jnp.exp(cum)).astype(BF16)
    chunk_decay = jnp.exp(last)
    for h in range(HGRN_HEADS):
        sl = slice(h * HGRN_DIM, (h + 1) * HGRN_DIM)
        v_h = v_ref[:, sl]
        att = jnp.where(causal, _dot_nt(q_rel[:, sl], k_rel[:, sl]), 0.0)
        st = st_ref[h]
        o = _dot(att.astype(BF16), v_h) + _dot_nt(q_dec[:, sl], st.astype(BF16))
        st_ref[h] = st * chunk_decay[:, sl] + _dot(v_h.astype(F32).T.astype(BF16), k_end[:, sl])
        on = o * lax.rsqrt(jnp.mean(o * o, axis=-1, keepdims=True) + EPS) * nw_ref[:, sl]
        o_ref[:, sl] = (on * _silu(g_ref[:, sl].astype(F32))).astype(o_ref.dtype)


def hgrn_mixer(qvg, f_raw, lb, norm_w):
    t = qvg.shape[0]
    rows = HGRN_ROWS
    d = D_MODEL
    fixed = lambda i: (0, 0)
    return pl.pallas_call(
        _hgrn_kernel,
        grid=(t // rows,),
        in_specs=[pl.BlockSpec((rows, d), lambda i: (i, 0)),
                  pl.BlockSpec((rows, d), lambda i: (i, 1)),
                  pl.BlockSpec((rows, d), lambda i: (i, 2)),
                  pl.BlockSpec((rows, d), lambda i: (i, 0)),
                  pl.BlockSpec((1, d), fixed), pl.BlockSpec((1, d), fixed)],
        out_specs=pl.BlockSpec((rows, d), lambda i: (i, 0)),
        out_shape=jax.ShapeDtypeStruct((t, d), BF16),
        scratch_shapes=[pltpu.VMEM((HGRN_HEADS, HGRN_DIM, HGRN_DIM), F32)],
        compiler_params=_params(1),
        name="hgrn_mixer",
    )(qvg, qvg, qvg, f_raw, lb, norm_w)


def _logf_cumsum_kernel(f_ref, b_ref, o_ref, ot_ref, carry_ref):
    rows = f_ref.shape[0]

    @pl.when(pl.program_id(0) == 0)
    def _():
        carry_ref[...] = jnp.zeros(carry_ref.shape, F32)

    log_f = -_softplus(-(f_ref[...] + b_ref[...]))
    c = _dot_f32(_tril(rows).astype(F32), log_f) + carry_ref[...]
    carry_ref[...] = c[rows - 1:rows, :]
    c2 = c * LOG2E
    o_ref[...] = c2
    ot_ref[...] = c2.T


def logf_cumsum(f_raw, b_f, rows=512):
    t = f_raw.shape[0]
    return pl.pallas_call(
        _logf_cumsum_kernel,
        grid=(t // rows,),
        in_specs=[pl.BlockSpec((rows, LANES), lambda i: (i, 0)), pl.BlockSpec((1, LANES), lambda i: (0, 0))],
        out_specs=[pl.BlockSpec((rows, LANES), lambda i: (i, 0)), pl.BlockSpec((LANES, rows), lambda i: (0, i))],
        out_shape=[jax.ShapeDtypeStruct((t, LANES), F32), jax.ShapeDtypeStruct((LANES, t), F32)],
        scratch_shapes=[pltpu.VMEM((1, LANES), F32)],
        compiler_params=_params(1),
        name="logf_cumsum",
    )(f_raw, b_f)


def _fox_kernel(q_ref, k_ref, v_ref, dkc_ref, dqr_ref, o_ref):
    tq, tk = FOX_Q_BLOCK, FOX_K_BLOCK
    half = tq // 2
    h = pl.program_id(0)
    qi = pl.program_id(1)
    q_start = pl.multiple_of(qi * tq, tq)
    q = q_ref[...]
    dq = dqr_ref[:, pl.ds(q_start, tq)]

    def step(start, width, lo, carry, diagonal):
        m, l, acc = carry
        k_b = k_ref[pl.ds(start, width), :]
        v_b = v_ref[pl.ds(start, width), :]
        lane = lax.broadcasted_iota(jnp.int32, (width, LANES), 1)
        dk = jnp.sum(jnp.where(lane == FOX_F_LANE0 + h, dkc_ref[pl.ds(start, width), :], 0.0),
                     axis=-1, keepdims=True)
        s = _dot_nt(k_b, q[lo:]) - dk
        if diagonal:
            key = lax.broadcasted_iota(jnp.int32, s.shape, 0)
            qry = lax.broadcasted_iota(jnp.int32, s.shape, 1)
            s = jnp.where(qry >= key, s, -jnp.inf)
        m_new = jnp.maximum(m, jnp.max(s, axis=0, keepdims=True) + dq[:, lo:])
        alpha = jnp.exp2(m - m_new)
        p = jnp.exp2(s - (m_new - dq[:, lo:]))
        l = alpha * l + jnp.sum(p, axis=0, keepdims=True)
        acc = alpha * acc + _dot_tn(v_b, p.astype(BF16))
        return m_new, l, acc

    init = (jnp.full((1, tq), -jnp.inf, F32), jnp.zeros((1, tq), F32), jnp.zeros((FOX_DIM, tq), F32))
    n_wide = lax.shift_right_logical(qi, 1)
    carry = lax.fori_loop(0, n_wide, lambda ki, c: step(pl.multiple_of(ki * tk, tk), tk, 0, c, False), init)
    odd_start = pl.multiple_of(n_wide * tk, tk)
    carry = lax.fori_loop(0, qi & 1, lambda _, c: step(odd_start, tq, 0, c, False), carry)
    carry = step(q_start, half, 0, carry, True)
    m, l, acc = carry
    m_hi, l_hi, acc_hi = step(pl.multiple_of(q_start + half, half), half, half,
                              (m[:, half:], l[:, half:], acc[:, half:]), True)
    l = jnp.concatenate([l[:, :half], l_hi], axis=1)
    acc = jnp.concatenate([acc[:, :half], acc_hi], axis=1)
    o_ref[...] = (acc / l).T.astype(o_ref.dtype)


def fox_attention(qkv, dcum, dcum_t):
    t = qkv.shape[0]
    tq = FOX_Q_BLOCK
    return pl.pallas_call(
        _fox_kernel,
        grid=(FOX_HEADS, t // tq),
        in_specs=[pl.BlockSpec((tq, FOX_DIM), lambda h, i: (i, h)),
                  pl.BlockSpec((t, FOX_DIM), lambda h, i: (0, FOX_HEADS + h)),
                  pl.BlockSpec((t, FOX_DIM), lambda h, i: (0, 2 * FOX_HEADS + h)),
                  pl.BlockSpec((t, LANES), lambda h, i: (0, 0)),
                  pl.BlockSpec((None, 1, t), lambda h, i: (h, 0, 0))],
        out_specs=pl.BlockSpec((tq, FOX_DIM), lambda h, i: (i, h)),
        out_shape=jax.ShapeDtypeStruct((t, FOX_HEADS * FOX_DIM), BF16),
        compiler_params=_params(2),
        name="fox_attention",
    )(qkv, qkv, qkv, dcum, dcum_t)


def _swiglu_half(h, hn, w_in, w_out, layer):
    n_main = (D_FF // FFN_TILE) * FFN_TILE
    act_main = ffn_in(hn, w_in, layer, 0, n_main, FFN_TILE)
    act_tail = ffn_in(hn, w_in, layer, n_main, D_FF - n_main, D_FF - n_main)
    return matmul_residual(h, w_out, layer, [(act_main, 0), (act_tail, n_main)], scale=0.5)


def _ssd_hgrn_layer(h, hn, w_in, layer, conv_w, conv_b, dt_bias, a_log, d_skip, ssd_norm_w, lb, hgrn_norm_w, w_out):
    d = D_MODEL
    nb = SSD_GROUPS * SSD_STATE
    o_dt = 2 * d + 2 * nb
    o_q = o_dt + SSD_HEADS
    tile = 1024 // SUBLANES
    wt = jnp.swapaxes(w_in, 1, 2)
    proj = matmul_nt(hn, wt, layer, BF16, o_dt, lambda j: j * tile)
    dt_raw = matmul_nt(hn, wt, layer, F32, LANES, lambda j: o_dt // SUBLANES, tn=LANES)
    first = o_q // SUBLANES
    qvg = matmul_nt(hn, wt, layer, BF16, 3 * d, lambda j: first + (j + jnp.where(j >= 2, 2, 0)) * tile)
    f_raw = matmul_nt(hn, wt, layer, F32, d, lambda j: first + (j + 2) * tile)
    pad_heads = lambda v: jnp.pad(v, (0, LANES - SSD_HEADS)).reshape(1, LANES)
    y_a = ssd_mixer(proj, dt_raw, conv_w[:, :d], conv_b[:d].reshape(1, d), conv_w[:, d:],
                    conv_b[d:].reshape(1, 2 * nb), pad_heads(dt_bias), pad_heads(a_log),
                    jnp.repeat(d_skip, SSD_HEAD_DIM).reshape(1, d), ssd_norm_w.reshape(1, d))
    y_b = hgrn_mixer(qvg, f_raw, lb.reshape(1, d), hgrn_norm_w.reshape(1, d))
    return matmul_residual(h, w_out, layer, [(y_a, 0), (y_b, d)], tm=1024)


def _fox_layer(h, hn, w_in, layer, b_f, w_out):
    d = D_MODEL
    t = h.shape[0]
    tile = 1024 // SUBLANES
    wt = jnp.swapaxes(w_in, 1, 2)
    qkv = matmul_nt(hn, wt, layer, BF16, 3 * d, lambda j: j * tile, scaled_blocks=d // 1024,
                    scale=LOG2E * FOX_DIM ** -0.5)
    f_raw = matmul_nt(hn, wt, layer, F32, LANES, lambda j: (3 * d + FOX_HEADS - LANES) // SUBLANES, tn=LANES)
    b_pad = jnp.pad(b_f, (FOX_F_LANE0, 0)).reshape(1, LANES)
    dcum, dcum_t = logf_cumsum(f_raw, b_pad)
    o = fox_attention(qkv, dcum, dcum_t[FOX_F_LANE0:].reshape(FOX_HEADS, 1, t))
    return matmul_residual(h, w_out, layer, [(o, 0)], tm=1024)


def kernel(x, p, ffn1_norm, ffn1_w_in, ffn1_w_out, mix_norm, ab_w_in, ssd_conv_w, ssd_conv_b, ssd_dt_bias,
           ssd_a_log, ssd_d, ssd_norm, hgrn_lb_logits, hgrn_norm, ab_w_out, fox_w_in, fox_b_f, fox_w_out,
           ffn2_norm, ffn2_w_in, ffn2_w_out, ple_gate_norm, ple_w_gate, ple_w_up, ple_norm, final_norm):
    bsz, t, d = x.shape
    depth = p.shape[0]
    assert bsz == 1 and d == D_MODEL
    lb_all = jnp.cumsum(jax.nn.softmax(hgrn_lb_logits.astype(F32), axis=0), axis=0)
    h = x.reshape(t, d)
    hn = rmsnorm(h, ffn1_norm[0], BF16)
    for i in range(depth):
        j = i // 2
        h = _swiglu_half(h, hn, ffn1_w_in, ffn1_w_out, i)
        hn = rmsnorm(h, mix_norm[i], BF16)
        if i % 2 == 0:
            h = _ssd_hgrn_layer(h, hn, ab_w_in, j, ssd_conv_w[j], ssd_conv_b[j], ssd_dt_bias[j], ssd_a_log[j],
                                ssd_d[j], ssd_norm[j], lb_all[i], hgrn_norm[j], ab_w_out)
        else:
            h = _fox_layer(h, hn, fox_w_in, j, fox_b_f[j], fox_w_out)
        hn = rmsnorm(h, ffn2_norm[i], BF16)
        h = _swiglu_half(h, hn, ffn2_w_in, ffn2_w_out, i)
        final = i == depth - 1
        next_w = final_norm if final else ffn1_norm[i + 1]
        outs = ple_add(h, p, ple_w_gate, ple_w_up, i, ple_gate_norm[i], ple_norm[i], next_w, final)
        if final:
            return outs[0].reshape(bsz, t, d)
        h, hn = outs
```

```python
import functools
import math

import jax
import jax.numpy as jnp
from jax import lax
from jax.experimental import pallas as pl
from jax.experimental.pallas import tpu as pltpu

F32 = jnp.float32
BF16 = jnp.bfloat16
EPS = 1e-6

D_MODEL = 2048
D_FF = 5504
SSD_HEADS = 32
SSD_HEAD_DIM = 64
SSD_GROUPS = 4
SSD_STATE = 128
SSD_GROUP_WIDTH = 512
SSD_CONV = 4
HGRN_HEADS = 16
HGRN_DIM = 128
FOX_HEADS = 16
FOX_DIM = 128
LANES = 128
SUBLANES = 8
CONV_HALO = 16

SSD_ROWS = 128
HGRN_ROWS = 64
HGRN_CHUNKS_PER_STEP = 4
FOX_Q_BLOCK = 1024
FOX_K_BLOCK = 2048
FOX_F_LANE0 = LANES - FOX_HEADS
FFN_TILE = 512
VMEM_LIMIT = 56 * 1024 * 1024
LOG2E = math.log2(math.e)


def _params(n_axes, vmem=VMEM_LIMIT):
    return pltpu.CompilerParams(dimension_semantics=("arbitrary",) * n_axes, vmem_limit_bytes=vmem)


def _silu(x):
    return x * jax.nn.sigmoid(x)


def _softplus(x):
    return jnp.maximum(x, 0.0) + jnp.log(1.0 + jnp.exp(-jnp.abs(x)))


def _dot(a, b):
    return jnp.dot(a, b, preferred_element_type=F32)


def _dot_nt(a, b):
    return lax.dot_general(a, b, (((1,), (1,)), ((), ())), preferred_element_type=F32)


def _dot_tn(a, b):
    return lax.dot_general(a, b, (((0,), (0,)), ((), ())), preferred_element_type=F32)


def _dot_f32(a, b):
    return jnp.dot(a, b, preferred_element_type=F32, precision=lax.Precision.HIGHEST)


def _tril(n):
    r = lax.broadcasted_iota(jnp.int32, (n, n), 0)
    c = lax.broadcasted_iota(jnp.int32, (n, n), 1)
    return r >= c


def _weight_window(layer, rows, width, row_tile, col_tile):
    return pl.BlockSpec((pl.Element(1), pl.Element(rows), pl.Element(width)),
                        lambda j, i: (layer, row_tile(j) * SUBLANES, col_tile(j) * LANES))


def _cast_weights_once(pairs):
    @pl.when(pl.program_id(1) == 0)
    def _():
        for w_ref, wb_ref in pairs:
            wb_ref[...] = w_ref[0].astype(BF16)


def _rmsnorm_kernel(h_ref, w_ref, o_ref):
    x = h_ref[...]
    y = x * lax.rsqrt(jnp.mean(x * x, axis=-1, keepdims=True) + EPS)
    o_ref[...] = (y * w_ref[...]).astype(o_ref.dtype)


def rmsnorm(h, w, out_dtype, tm=512):
    m, d = h.shape
    return pl.pallas_call(
        _rmsnorm_kernel,
        grid=(m // tm,),
        in_specs=[pl.BlockSpec((tm, d), lambda i: (i, 0)), pl.BlockSpec((1, d), lambda i: (0, 0))],
        out_specs=pl.BlockSpec((tm, d), lambda i: (i, 0)),
        out_shape=jax.ShapeDtypeStruct((m, d), out_dtype),
        compiler_params=_params(1),
        name="rmsnorm",
    )(h, w.reshape(1, d))


def _mm_nt_kernel(x_ref, w_ref, o_ref, wb_ref, *, scaled_blocks, scale):
    _cast_weights_once([(w_ref, wb_ref)])
    acc = _dot_nt(x_ref[...], wb_ref[...])
    if scaled_blocks:
        acc = acc * jnp.where(pl.program_id(0) < scaled_blocks, scale, 1.0)
    o_ref[...] = acc.astype(o_ref.dtype)


def matmul_nt(x, wt, layer, out_dtype, n_out, row_tile, tn=1024, tm=1024, scaled_blocks=0, scale=1.0):
    m, k = x.shape
    return pl.pallas_call(
        functools.partial(_mm_nt_kernel, scaled_blocks=scaled_blocks, scale=scale),
        grid=(n_out // tn, m // tm),
        in_specs=[pl.BlockSpec((tm, k), lambda j, i: (i, 0)), _weight_window(layer, tn, k, row_tile, lambda j: 0)],
        out_specs=pl.BlockSpec((tm, tn), lambda j, i: (i, j)),
        out_shape=jax.ShapeDtypeStruct((m, n_out), out_dtype),
        scratch_shapes=[pltpu.VMEM((tn, k), BF16)],
        compiler_params=_params(2),
        name="matmul_nt",
    )(x, wt)


def _ffn_in_kernel(x_ref, wg_ref, wu_ref, o_ref, wgb_ref, wub_ref):
    _cast_weights_once([(wg_ref, wgb_ref), (wu_ref, wub_ref)])
    x = x_ref[...]
    gate = _dot(x, wgb_ref[...])
    up = _dot(x, wub_ref[...])
    o_ref[...] = (_silu(gate) * up).astype(o_ref.dtype)


def ffn_in(x, w_in, layer, col0, n_out, tn, tm=1024):
    m, k = x.shape
    blocks_per_tile = tn // LANES
    gate_col = lambda j: col0 // LANES + j * blocks_per_tile
    up_col = lambda j: (D_FF + col0) // LANES + j * blocks_per_tile
    return pl.pallas_call(
        _ffn_in_kernel,
        grid=(n_out // tn, m // tm),
        in_specs=[pl.BlockSpec((tm, k), lambda j, i: (i, 0)),
                  _weight_window(layer, k, tn, lambda j: 0, gate_col),
                  _weight_window(layer, k, tn, lambda j: 0, up_col)],
        out_specs=pl.BlockSpec((tm, tn), lambda j, i: (i, j)),
        out_shape=jax.ShapeDtypeStruct((m, n_out), BF16),
        scratch_shapes=[pltpu.VMEM((k, tn), BF16), pltpu.VMEM((k, tn), BF16)],
        compiler_params=_params(2),
        name="ffn_in",
    )(x, w_in, w_in)


def _mm_resid_kernel(*refs, n_pairs, scale):
    h_ref = refs[0]
    a_refs = refs[1:1 + n_pairs]
    w_refs = refs[1 + n_pairs:1 + 2 * n_pairs]
    o_ref = refs[1 + 2 * n_pairs]
    wb_refs = refs[2 + 2 * n_pairs:]
    _cast_weights_once(list(zip(w_refs, wb_refs)))
    acc = _dot(a_refs[0][...], wb_refs[0][...])
    for a_ref, wb_ref in zip(a_refs[1:], wb_refs[1:]):
        acc = acc + _dot(a_ref[...], wb_ref[...])
    if scale != 1.0:
        acc = scale * acc
    o_ref[...] = h_ref[...] + acc


def matmul_residual(h, w, layer, pieces, scale=1.0, tm=512, tn=512):
    m, n = h.shape
    blocks_per_tile = tn // LANES
    a_specs = [pl.BlockSpec((tm, a.shape[1]), lambda j, i: (i, 0)) for a, _ in pieces]
    w_specs = [_weight_window(layer, a.shape[1], tn, lambda j, r=row0 // SUBLANES: r, lambda j: j * blocks_per_tile)
               for a, row0 in pieces]
    return pl.pallas_call(
        functools.partial(_mm_resid_kernel, n_pairs=len(pieces), scale=scale),
        grid=(n // tn, m // tm),
        in_specs=[pl.BlockSpec((tm, tn), lambda j, i: (i, j))] + a_specs + w_specs,
        out_specs=pl.BlockSpec((tm, tn), lambda j, i: (i, j)),
        out_shape=jax.ShapeDtypeStruct((m, n), F32),
        scratch_shapes=[pltpu.VMEM((a.shape[1], tn), BF16) for a, _ in pieces],
        compiler_params=_params(2),
        name="matmul_residual",
    )(h, *[a for a, _ in pieces], *[w] * len(pieces))


def _ple_kernel(h_ref, p_ref, wg_ref, wu_ref, gn_ref, pn_ref, nn_ref, *refs, final):
    wgb_ref, wub_ref = refs[-2:]
    out_refs = refs[:-2]

    @pl.when(pl.program_id(0) == 0)
    def _():
        wgb_ref[...] = wg_ref[...].astype(BF16)
        wub_ref[...] = wu_ref[...].astype(BF16)

    def normed(x, w_ref):
        return x * lax.rsqrt(jnp.mean(x * x, axis=-1, keepdims=True) + EPS) * w_ref[...]

    h = h_ref[...]
    gate = jax.nn.sigmoid(_dot(normed(h, gn_ref).astype(BF16), wgb_ref[...]))
    emb = normed(_dot(p_ref[...].astype(BF16), wub_ref[...]), pn_ref)
    h = h + emb * gate
    if not final:
        out_refs[0][...] = h
    out_refs[-1][...] = normed(h, nn_ref).astype(out_refs[-1].dtype)


def ple_add(h, p, w_gate, w_up, layer, gate_norm_w, post_norm_w, next_norm_w, final, tm=256):
    m, d = h.shape
    pd = p.shape[-1]
    row = lambda i: (i, 0)
    fixed = lambda i: (0, 0)
    resident = functools.partial(pl.BlockSpec, index_map=lambda i: (layer, 0, 0), pipeline_mode=pl.Buffered(1))
    out_dtypes = [F32] if final else [F32, BF16]
    return pl.pallas_call(
        functools.partial(_ple_kernel, final=final),
        grid=(m // tm,),
        in_specs=[pl.BlockSpec((tm, d), row),
                  pl.BlockSpec((None, None, tm, pd), lambda i: (layer, 0, i, 0)),
                  resident((None, d, d)), resident((None, pd, d)),
                  pl.BlockSpec((1, d), fixed), pl.BlockSpec((1, d), fixed), pl.BlockSpec((1, d), fixed)],
        out_specs=[pl.BlockSpec((tm, d), row) for _ in out_dtypes],
        out_shape=[jax.ShapeDtypeStruct((m, d), dt) for dt in out_dtypes],
        scratch_shapes=[pltpu.VMEM((d, d), BF16), pltpu.VMEM((pd, d), BF16)],
        compiler_params=_params(1),
        name="ple_add",
    )(h, p, w_gate, w_up, gate_norm_w.reshape(1, d), post_norm_w.reshape(1, d), next_norm_w.reshape(1, d))


def _ssd_kernel(z_ref, xs_ref, bc_ref, dt_ref, cwx_ref, cbx_ref, cwbc_ref, cbbc_ref, dtb_ref, alog_ref,
                dskip_ref, nw_ref, o_ref, xhalo_ref, bchalo_ref, st_ref, spread_ref):
    rows = SSD_ROWS

    @pl.when(pl.program_id(0) == 0)
    def _():
        xhalo_ref[...] = jnp.zeros(xhalo_ref.shape, BF16)
        bchalo_ref[...] = jnp.zeros(bchalo_ref.shape, BF16)
        st_ref[...] = jnp.zeros(st_ref.shape, F32)
        head = lax.broadcasted_iota(jnp.int32, spread_ref.shape, 0)
        lane = lax.broadcasted_iota(jnp.int32, spread_ref.shape, 1)
        spread_ref[...] = jnp.where(lane // SSD_HEAD_DIM == head, 1.0, 0.0).astype(BF16)

    out_row = lax.broadcasted_iota(jnp.int32, (rows, CONV_HALO + rows), 0)
    in_row = lax.broadcasted_iota(jnp.int32, (rows, CONV_HALO + rows), 1)
    shifts = [jnp.where(in_row == out_row + CONV_HALO - back, 1.0, 0.0).astype(BF16)
              for back in range(SSD_CONV - 1, 0, -1)]

    def conv_silu(halo_ref, raw_ref, w_ref, b_ref):
        raw = raw_ref[...]
        ext = jnp.concatenate([halo_ref[...], raw], axis=0)
        acc = b_ref[...] + raw.astype(F32) * w_ref[SSD_CONV - 1:SSD_CONV, :]
        for k, shift in enumerate(shifts):
            acc = acc + _dot(shift, ext) * w_ref[k:k + 1, :]
        halo_ref[...] = raw[rows - CONV_HALO:, :]
        return _silu(acc)

    xs = conv_silu(xhalo_ref, xs_ref, cwx_ref, cbx_ref)
    bc = conv_silu(bchalo_ref, bc_ref, cwbc_ref, cbbc_ref)

    dt = _softplus(dt_ref[...] + dtb_ref[...])
    a = -jnp.exp(alog_ref[...])
    causal = _tril(rows)
    cum = _dot_f32(causal.astype(F32), dt * a)
    cum_t = cum.T
    low_half = lax.broadcasted_iota(jnp.int32, (rows, LANES), 1) < SSD_HEAD_DIM

    def per_lane(v, terms):
        out, rest = None, v
        for _ in range(terms):
            piece = rest.astype(BF16)
            rest = rest - piece.astype(F32)
            moved = _dot(piece, spread_ref[...])
            out = moved if out is None else out + moved
        return out

    dt_x = per_lane(dt, 2)
    cum_x = per_lane(cum, 3)

    pairs_per_group = SSD_GROUP_WIDTH // LANES
    for g in range(SSD_GROUPS):
        b_g = bc[:, g * SSD_STATE:(g + 1) * SSD_STATE]
        c_g = bc[:, (SSD_GROUPS + g) * SSD_STATE:(SSD_GROUPS + g + 1) * SSD_STATE].astype(BF16)
        cb = _dot_nt(c_g, b_g.astype(BF16))
        st = st_ref[g]
        y_off = _dot(c_g, st.astype(BF16))
        y_tiles, xw_tiles, dec_tiles = [], [], []
        for jj in range(pairs_per_group):
            j = g * pairs_per_group + jj
            h0 = 2 * j
            sl = slice(j * LANES, (j + 1) * LANES)
            x_p = xs[:, sl]
            cum_p = cum_x[:, sl]
            xdt = x_p * dt_x[:, sl]
            xdt_b = xdt.astype(BF16)
            halves = []
            for h in (h0, h0 + 1):
                diff = cum[:, h:h + 1] - cum_t[h:h + 1, :]
                decay = jnp.exp(jnp.where(causal, diff, -jnp.inf))
                halves.append(_dot((cb * decay).astype(BF16), xdt_b))
            y_diag = jnp.where(low_half, halves[0], halves[1])
            cum_last = cum_p[rows - 1:rows, :]
            xw_tiles.append((xdt * jnp.exp(cum_last - cum_p)).astype(BF16))
            dec_tiles.append(jnp.exp(cum_last))
            y = y_diag + y_off[:, jj * LANES:(jj + 1) * LANES] * jnp.exp(cum_p) + x_p * dskip_ref[:, sl]
            y_tiles.append(y * _silu(z_ref[:, sl].astype(F32)))
        xw = jnp.concatenate(xw_tiles, axis=1)
        st_ref[g] = st * jnp.concatenate(dec_tiles, axis=1) + _dot(b_g.T.astype(BF16), xw)
        ss = jnp.sum(y_tiles[0] * y_tiles[0], axis=-1, keepdims=True)
        for t in y_tiles[1:]:
            ss = ss + jnp.sum(t * t, axis=-1, keepdims=True)
        inv = lax.rsqrt(ss * (1.0 / SSD_GROUP_WIDTH) + EPS)
        for jj in range(pairs_per_group):
            sl = slice((g * pairs_per_group + jj) * LANES, (g * pairs_per_group + jj + 1) * LANES)
            o_ref[:, sl] = (y_tiles[jj] * inv * nw_ref[:, sl]).astype(o_ref.dtype)


def ssd_mixer(proj, dt_raw, conv_wx, conv_bx, conv_wbc, conv_bbc, dt_bias, a_log, d_skip_x, norm_w):
    t = proj.shape[0]
    rows = SSD_ROWS
    d = D_MODEL
    nbc = 2 * SSD_GROUPS * SSD_STATE
    fixed = lambda i: (0, 0)
    return pl.pallas_call(
        _ssd_kernel,
        grid=(t // rows,),
        in_specs=[pl.BlockSpec((rows, d), lambda i: (i, 0)),
                  pl.BlockSpec((rows, d), lambda i: (i, 1)),
                  pl.BlockSpec((rows, nbc), lambda i: (i, 2 * d // nbc)),
                  pl.BlockSpec((rows, LANES), lambda i: (i, 0)),
                  pl.BlockSpec((SSD_CONV, d), fixed), pl.BlockSpec((1, d), fixed),
                  pl.BlockSpec((SSD_CONV, nbc), fixed), pl.BlockSpec((1, nbc), fixed),
                  pl.BlockSpec((1, LANES), fixed), pl.BlockSpec((1, LANES), fixed),
                  pl.BlockSpec((1, d), fixed), pl.BlockSpec((1, d), fixed)],
        out_specs=pl.BlockSpec((rows, d), lambda i: (i, 0)),
        out_shape=jax.ShapeDtypeStruct((t, d), BF16),
        scratch_shapes=[pltpu.VMEM((CONV_HALO, d), BF16),
                        pltpu.VMEM((CONV_HALO, nbc), BF16),
                        pltpu.VMEM((SSD_GROUPS, SSD_STATE, SSD_GROUP_WIDTH), F32),
                        pltpu.VMEM((LANES, d), BF16)],
        compiler_params=_params(1),
        name="ssd_mixer",
    )(proj, proj, proj, dt_raw, conv_wx, conv_bx, conv_wbc, conv_bbc, dt_bias, a_log, d_skip_x, norm_w)


def _hgrn_kernel(q_ref, v_ref, g_ref, f_ref, lb_ref, nw_ref, o_ref, st_ref):
    rows = HGRN_ROWS

    @pl.when(pl.program_id(0) == 0)
    def _():
        st_ref[...] = jnp.zeros(st_ref.shape, F32)

    lb = lb_ref[...]
    causal = _tril(rows)
    tri = causal.astype(F32)
    for c in range(HGRN_CHUNKS_PER_STEP):
        r = slice(c * rows, (c + 1) * rows)
        f = lb + (1.0 - lb) * jax.nn.sigmoid(f_ref[r, :])
        k = 1.0 - f
        cum = _dot_f32(tri, jnp.log(f))
        qf = _silu(q_ref[r, :].astype(F32))
        mid = cum[rows // 2 - 1:rows // 2, :]
        last = cum[rows - 1:rows, :]
        q_rel = (qf * jnp.exp(cum - mid)).astype(BF16)
        k_rel = (k * jnp.exp(mid - cum)).astype(BF16)
        k_end = (k * jnp.exp(last - cum)).astype(BF16)
        q_dec = (qf * jnp.exp(cum)).astype(BF16)
        chunk_decay = jnp.exp(last)
        for h in range(HGRN_HEADS):
            sl = slice(h * HGRN_DIM, (h + 1) * HGRN_DIM)
            v_h = v_ref[r, sl]
            att = jnp.where(causal, _dot_nt(q_rel[:, sl], k_rel[:, sl]), 0.0)
            st = st_ref[h]
            o = _dot(att.astype(BF16), v_h) + _dot_nt(q_dec[:, sl], st.astype(BF16))
            st_ref[h] = st * chunk_decay[:, sl] + _dot_tn(v_h, k_end[:, sl])
            on = o * lax.rsqrt(jnp.mean(o * o, axis=-1, keepdims=True) + EPS) * nw_ref[:, sl]
            o_ref[r, sl] = (on * _silu(g_ref[r, sl].astype(F32))).astype(o_ref.dtype)


def hgrn_mixer(qvg, f_raw, lb, norm_w):
    t = qvg.shape[0]
    rows = HGRN_ROWS * HGRN_CHUNKS_PER_STEP
    d = D_MODEL
    fixed = lambda i: (0, 0)
    return pl.pallas_call(
        _hgrn_kernel,
        grid=(t // rows,),
        in_specs=[pl.BlockSpec((rows, d), lambda i: (i, 0)),
                  pl.BlockSpec((rows, d), lambda i: (i, 1)),
                  pl.BlockSpec((rows, d), lambda i: (i, 2)),
                  pl.BlockSpec((rows, d), lambda i: (i, 0)),
                  pl.BlockSpec((1, d), fixed), pl.BlockSpec((1, d), fixed)],
        out_specs=pl.BlockSpec((rows, d), lambda i: (i, 0)),
        out_shape=jax.ShapeDtypeStruct((t, d), BF16),
        scratch_shapes=[pltpu.VMEM((HGRN_HEADS, HGRN_DIM, HGRN_DIM), F32)],
        compiler_params=_params(1),
        name="hgrn_mixer",
    )(qvg, qvg, qvg, f_raw, lb, norm_w)


def _logf_cumsum_kernel(f_ref, b_ref, o_ref, ot_ref, carry_ref):
    rows = f_ref.shape[0]

    @pl.when(pl.program_id(0) == 0)
    def _():
        carry_ref[...] = jnp.zeros(carry_ref.shape, F32)

    log_f = -_softplus(-(f_ref[...] + b_ref[...]))
    c = _dot_f32(_tril(rows).astype(F32), log_f) + carry_ref[...]
    carry_ref[...] = c[rows - 1:rows, :]
    c2 = c * LOG2E
    o_ref[...] = c2
    ot_ref[...] = c2.T


def logf_cumsum(f_raw, b_f, rows=512):
    t = f_raw.shape[0]
    return pl.pallas_call(
        _logf_cumsum_kernel,
        grid=(t // rows,),
        in_specs=[pl.BlockSpec((rows, LANES), lambda i: (i, 0)), pl.BlockSpec((1, LANES), lambda i: (0, 0))],
        out_specs=[pl.BlockSpec((rows, LANES), lambda i: (i, 0)), pl.BlockSpec((LANES, rows), lambda i: (0, i))],
        out_shape=[jax.ShapeDtypeStruct((t, LANES), F32), jax.ShapeDtypeStruct((LANES, t), F32)],
        scratch_shapes=[pltpu.VMEM((1, LANES), F32)],
        compiler_params=_params(1),
        name="logf_cumsum",
    )(f_raw, b_f)


def _fox_kernel(q_ref, k_ref, v_ref, dkc_ref, dqr_ref, o_ref):
    tq, tk = FOX_Q_BLOCK, FOX_K_BLOCK
    half = tq // 2
    h = pl.program_id(0)
    qi = pl.program_id(1)
    q_start = pl.multiple_of(qi * tq, tq)
    q = q_ref[...]
    dq = dqr_ref[:, pl.ds(q_start, tq)]

    def step(start, width, lo, carry, diagonal):
        m, l, acc = carry
        k_b = k_ref[pl.ds(start, width), :]
        v_b = v_ref[pl.ds(start, width), :]
        lane = lax.broadcasted_iota(jnp.int32, (width, LANES), 1)
        dk = jnp.sum(jnp.where(lane == FOX_F_LANE0 + h, dkc_ref[pl.ds(start, width), :], 0.0),
                     axis=-1, keepdims=True)
        s = _dot_nt(k_b, q[lo:]) - dk
        if diagonal:
            key = lax.broadcasted_iota(jnp.int32, s.shape, 0)
            qry = lax.broadcasted_iota(jnp.int32, s.shape, 1)
            s = jnp.where(qry >= key, s, -jnp.inf)
        m_new = jnp.maximum(m, jnp.max(s, axis=0, keepdims=True) + dq[:, lo:])
        alpha = jnp.exp2(m - m_new)
        p = jnp.exp2(s - (m_new - dq[:, lo:]))
        l = alpha * l + jnp.sum(p, axis=0, keepdims=True)
        acc = alpha * acc + _dot_tn(v_b, p.astype(BF16))
        return m_new, l, acc

    init = (jnp.full((1, tq), -jnp.inf, F32), jnp.zeros((1, tq), F32), jnp.zeros((FOX_DIM, tq), F32))
    n_wide = lax.shift_right_logical(qi, 1)
    carry = lax.fori_loop(0, n_wide, lambda ki, c: step(pl.multiple_of(ki * tk, tk), tk, 0, c, False), init)
    odd_start = pl.multiple_of(n_wide * tk, tk)
    carry = lax.fori_loop(0, qi & 1, lambda _, c: step(odd_start, tq, 0, c, False), carry)
    carry = step(q_start, half, 0, carry, True)
    m, l, acc = carry
    m_hi, l_hi, acc_hi = step(pl.multiple_of(q_start + half, half), half, half,
                              (m[:, half:], l[:, half:], acc[:, half:]), True)
    l = jnp.concatenate([l[:, :half], l_hi], axis=1)
    acc = jnp.concatenate([acc[:, :half], acc_hi], axis=1)
    o_ref[...] = (acc / l).T.astype(o_ref.dtype)


def fox_attention(qkv, dcum, dcum_t):
    t = qkv.shape[0]
    tq = FOX_Q_BLOCK
    return pl.pallas_call(
        _fox_kernel,
        grid=(FOX_HEADS, t // tq),
        in_specs=[pl.BlockSpec((tq, FOX_DIM), lambda h, i: (i, h)),
                  pl.BlockSpec((t, FOX_DIM), lambda h, i: (0, FOX_HEADS + h)),
                  pl.BlockSpec((t, FOX_DIM), lambda h, i: (0, 2 * FOX_HEADS + h)),
                  pl.BlockSpec((t, LANES), lambda h, i: (0, 0)),
                  pl.BlockSpec((None, 1, t), lambda h, i: (h, 0, 0))],
        out_specs=pl.BlockSpec((tq, FOX_DIM), lambda h, i: (i, h)),
        out_shape=jax.ShapeDtypeStruct((t, FOX_HEADS * FOX_DIM), BF16),
        compiler_params=_params(2),
        name="fox_attention",
    )(qkv, qkv, qkv, dcum, dcum_t)


def _swiglu_half(h, hn, w_in, w_out, layer):
    n_main = (D_FF // FFN_TILE) * FFN_TILE
    act_main = ffn_in(hn, w_in, layer, 0, n_main, FFN_TILE)
    act_tail = ffn_in(hn, w_in, layer, n_main, D_FF - n_main, D_FF - n_main)
    return matmul_residual(h, w_out, layer, [(act_main, 0), (act_tail, n_main)], scale=0.5)


def _ssd_hgrn_layer(h, hn, w_in, layer, conv_w, conv_b, dt_bias, a_log, d_skip, ssd_norm_w, lb, hgrn_norm_w, w_out):
    d = D_MODEL
    nb = SSD_GROUPS * SSD_STATE
    o_dt = 2 * d + 2 * nb
    o_q = o_dt + SSD_HEADS
    tile = 1024 // SUBLANES
    wt = jnp.swapaxes(w_in, 1, 2)
    proj = matmul_nt(hn, wt, layer, BF16, o_dt, lambda j: j * tile)
    dt_raw = matmul_nt(hn, wt, layer, F32, LANES, lambda j: o_dt // SUBLANES, tn=LANES)
    first = o_q // SUBLANES
    qvg = matmul_nt(hn, wt, layer, BF16, 3 * d, lambda j: first + (j + jnp.where(j >= 2, 2, 0)) * tile)
    f_raw = matmul_nt(hn, wt, layer, F32, d, lambda j: first + (j + 2) * tile)
    pad_heads = lambda v: jnp.pad(v, (0, LANES - SSD_HEADS)).reshape(1, LANES)
    y_a = ssd_mixer(proj, dt_raw, conv_w[:, :d], conv_b[:d].reshape(1, d), conv_w[:, d:],
                    conv_b[d:].reshape(1, 2 * nb), pad_heads(dt_bias), pad_heads(a_log),
                    jnp.repeat(d_skip, SSD_HEAD_DIM).reshape(1, d), ssd_norm_w.reshape(1, d))
    y_b = hgrn_mixer(qvg, f_raw, lb.reshape(1, d), hgrn_norm_w.reshape(1, d))
    return matmul_residual(h, w_out, layer, [(y_a, 0), (y_b, d)], tm=1024)


def _fox_layer(h, hn, w_in, layer, b_f, w_out):
    d = D_MODEL
    t = h.shape[0]
    tile = 1024 // SUBLANES
    wt = jnp.swapaxes(w_in, 1, 2)
    qkv = matmul_nt(hn, wt, layer, BF16, 3 * d, lambda j: j * tile, scaled_blocks=d // 1024,
                    scale=LOG2E * FOX_DIM ** -0.5)
    f_raw = matmul_nt(hn, wt, layer, F32, LANES, lambda j: (3 * d + FOX_HEADS - LANES) // SUBLANES, tn=LANES)
    b_pad = jnp.pad(b_f, (FOX_F_LANE0, 0)).reshape(1, LANES)
    dcum, dcum_t = logf_cumsum(f_raw, b_pad)
    o = fox_attention(qkv, dcum, dcum_t[FOX_F_LANE0:].reshape(FOX_HEADS, 1, t))
    return matmul_residual(h, w_out, layer, [(o, 0)], tm=1024)


def kernel(x, p, ffn1_norm, ffn1_w_in, ffn1_w_out, mix_norm, ab_w_in, ssd_conv_w, ssd_conv_b, ssd_dt_bias,
           ssd_a_log, ssd_d, ssd_norm, hgrn_lb_logits, hgrn_norm, ab_w_out, fox_w_in, fox_b_f, fox_w_out,
           ffn2_norm, ffn2_w_in, ffn2_w_out, ple_gate_norm, ple_w_gate, ple_w_up, ple_norm, final_norm):
    bsz, t, d = x.shape
    depth = p.shape[0]
    assert bsz == 1 and d == D_MODEL
    lb_all = jnp.cumsum(jax.nn.softmax(hgrn_lb_logits.astype(F32), axis=0), axis=0)
    h = x.reshape(t, d)
    hn = rmsnorm(h, ffn1_norm[0], BF16)
    for i in range(depth):
        j = i // 2
        h = _swiglu_half(h, hn, ffn1_w_in, ffn1_w_out, i)
        hn = rmsnorm(h, mix_norm[i], BF16)
        if i % 2 == 0:
            h = _ssd_hgrn_layer(h, hn, ab_w_in, j, ssd_conv_w[j], ssd_conv_b[j], ssd_dt_bias[j], ssd_a_log[j],
                                ssd_d[j], ssd_norm[j], lb_all[i], hgrn_norm[j], ab_w_out)
        else:
            h = _fox_layer(h, hn, fox_w_in, j, fox_b_f[j], fox_w_out)
        hn = rmsnorm(h, ffn2_norm[i], BF16)
        h = _swiglu_half(h, hn, ffn2_w_in, ffn2_w_out, i)
        final = i == depth - 1
        next_w = final_norm if final else ffn1_norm[i + 1]
        outs = ple_add(h, p, ple_w_gate, ple_w_up, i, ple_gate_norm[i], ple_norm[i], next_w, final)
        if final:
            return outs[0].reshape(bsz, t, d)
        h, hn = outs
```

```python
import functools
import math

import jax
import jax.numpy as jnp
from jax import lax
from jax.experimental import pallas as pl
from jax.experimental.pallas import tpu as pltpu

F32 = jnp.float32
BF16 = jnp.bfloat16
EPS = 1e-6

D_MODEL = 2048
D_FF = 5504
SSD_HEADS = 32
SSD_HEAD_DIM = 64
SSD_GROUPS = 4
SSD_STATE = 128
SSD_GROUP_WIDTH = 512
SSD_CONV = 4
HGRN_HEADS = 16
HGRN_DIM = 128
FOX_HEADS = 16
FOX_DIM = 128
LANES = 128
SUBLANES = 8
CONV_HALO = 16

SSD_ROWS = 128
HGRN_ROWS = 64
HGRN_CHUNKS_PER_STEP = 4
FOX_Q_BLOCK = 1024
FOX_K_BLOCK = 2048
FOX_PACK = 2
FOX_F_LANE0 = LANES - FOX_HEADS
FFN_TILE = 512
VMEM_LIMIT = 56 * 1024 * 1024
LOG2E = math.log2(math.e)


def _params(n_axes, vmem=VMEM_LIMIT):
    return pltpu.CompilerParams(dimension_semantics=("arbitrary",) * n_axes, vmem_limit_bytes=vmem)


def _silu(x):
    return x * jax.nn.sigmoid(x)


def _softplus(x):
    return jnp.maximum(x, 0.0) + jnp.log(1.0 + jnp.exp(-jnp.abs(x)))


def _dot(a, b):
    return jnp.dot(a, b, preferred_element_type=F32)


def _dot_nt(a, b):
    return lax.dot_general(a, b, (((1,), (1,)), ((), ())), preferred_element_type=F32)


def _dot_tn(a, b):
    return lax.dot_general(a, b, (((0,), (0,)), ((), ())), preferred_element_type=F32)


def _dot_f32(a, b):
    return jnp.dot(a, b, preferred_element_type=F32, precision=lax.Precision.HIGHEST)


def _tril(n):
    r = lax.broadcasted_iota(jnp.int32, (n, n), 0)
    c = lax.broadcasted_iota(jnp.int32, (n, n), 1)
    return r >= c


def _weight_window(layer, rows, width, row_tile, col_tile):
    return pl.BlockSpec((pl.Element(1), pl.Element(rows), pl.Element(width)),
                        lambda j, i: (layer, row_tile(j) * SUBLANES, col_tile(j) * LANES))


def _cast_weights_once(pairs):
    @pl.when(pl.program_id(1) == 0)
    def _():
        for w_ref, wb_ref in pairs:
            wb_ref[...] = w_ref[0].astype(BF16)


def _rmsnorm_kernel(h_ref, w_ref, o_ref):
    x = h_ref[...]
    y = x * lax.rsqrt(jnp.mean(x * x, axis=-1, keepdims=True) + EPS)
    o_ref[...] = (y * w_ref[...]).astype(o_ref.dtype)


def rmsnorm(h, w, out_dtype, tm=512):
    m, d = h.shape
    return pl.pallas_call(
        _rmsnorm_kernel,
        grid=(m // tm,),
        in_specs=[pl.BlockSpec((tm, d), lambda i: (i, 0)), pl.BlockSpec((1, d), lambda i: (0, 0))],
        out_specs=pl.BlockSpec((tm, d), lambda i: (i, 0)),
        out_shape=jax.ShapeDtypeStruct((m, d), out_dtype),
        compiler_params=_params(1),
        name="rmsnorm",
    )(h, w.reshape(1, d))


def _mm_nt_kernel(x_ref, w_ref, o_ref, wb_ref, *, scaled_blocks, scale):
    _cast_weights_once([(w_ref, wb_ref)])
    acc = _dot_nt(x_ref[...], wb_ref[...])
    if scaled_blocks:
        acc = acc * jnp.where(pl.program_id(0) < scaled_blocks, scale, 1.0)
    o_ref[...] = acc.astype(o_ref.dtype)


def matmul_nt(x, wt, layer, out_dtype, n_out, row_tile, tn=1024, tm=1024, scaled_blocks=0, scale=1.0):
    m, k = x.shape
    return pl.pallas_call(
        functools.partial(_mm_nt_kernel, scaled_blocks=scaled_blocks, scale=scale),
        grid=(n_out // tn, m // tm),
        in_specs=[pl.BlockSpec((tm, k), lambda j, i: (i, 0)), _weight_window(layer, tn, k, row_tile, lambda j: 0)],
        out_specs=pl.BlockSpec((tm, tn), lambda j, i: (i, j)),
        out_shape=jax.ShapeDtypeStruct((m, n_out), out_dtype),
        scratch_shapes=[pltpu.VMEM((tn, k), BF16)],
        compiler_params=_params(2),
        name="matmul_nt",
    )(x, wt)


def _ffn_in_kernel(x_ref, wg_ref, wu_ref, o_ref, wgb_ref, wub_ref):
    _cast_weights_once([(wg_ref, wgb_ref), (wu_ref, wub_ref)])
    x = x_ref[...]
    gate = _dot(x, wgb_ref[...])
    up = _dot(x, wub_ref[...])
    o_ref[...] = (_silu(gate) * up).astype(o_ref.dtype)


def ffn_in(x, w_in, layer, col0, n_out, tn, tm=1024):
    m, k = x.shape
    blocks_per_tile = tn // LANES
    gate_col = lambda j: col0 // LANES + j * blocks_per_tile
    up_col = lambda j: (D_FF + col0) // LANES + j * blocks_per_tile
    return pl.pallas_call(
        _ffn_in_kernel,
        grid=(n_out // tn, m // tm),
        in_specs=[pl.BlockSpec((tm, k), lambda j, i: (i, 0)),
                  _weight_window(layer, k, tn, lambda j: 0, gate_col),
                  _weight_window(layer, k, tn, lambda j: 0, up_col)],
        out_specs=pl.BlockSpec((tm, tn), lambda j, i: (i, j)),
        out_shape=jax.ShapeDtypeStruct((m, n_out), BF16),
        scratch_shapes=[pltpu.VMEM((k, tn), BF16), pltpu.VMEM((k, tn), BF16)],
        compiler_params=_params(2),
        name="ffn_in",
    )(x, w_in, w_in)


def _mm_resid_kernel(*refs, n_pairs, scale):
    h_ref = refs[0]
    a_refs = refs[1:1 + n_pairs]
    w_refs = refs[1 + n_pairs:1 + 2 * n_pairs]
    o_ref = refs[1 + 2 * n_pairs]
    wb_refs = refs[2 + 2 * n_pairs:]
    _cast_weights_once(list(zip(w_refs, wb_refs)))
    acc = _dot(a_refs[0][...], wb_refs[0][...])
    for a_ref, wb_ref in zip(a_refs[1:], wb_refs[1:]):
        acc = acc + _dot(a_ref[...], wb_ref[...])
    if scale != 1.0:
        acc = scale * acc
    o_ref[...] = h_ref[...] + acc


def matmul_residual(h, w, layer, pieces, scale=1.0, tm=512, tn=512):
    m, n = h.shape
    blocks_per_tile = tn // LANES
    a_specs = [pl.BlockSpec((tm, a.shape[1]), lambda j, i: (i, 0)) for a, _ in pieces]
    w_specs = [_weight_window(layer, a.shape[1], tn, lambda j, r=row0 // SUBLANES: r, lambda j: j * blocks_per_tile)
               for a, row0 in pieces]
    return pl.pallas_call(
        functools.partial(_mm_resid_kernel, n_pairs=len(pieces), scale=scale),
        grid=(n // tn, m // tm),
        in_specs=[pl.BlockSpec((tm, tn), lambda j, i: (i, j))] + a_specs + w_specs,
        out_specs=pl.BlockSpec((tm, tn), lambda j, i: (i, j)),
        out_shape=jax.ShapeDtypeStruct((m, n), F32),
        scratch_shapes=[pltpu.VMEM((a.shape[1], tn), BF16) for a, _ in pieces],
        compiler_params=_params(2),
        name="matmul_residual",
    )(h, *[a for a, _ in pieces], *[w] * len(pieces))


def _ple_kernel(h_ref, p_ref, wg_ref, wu_ref, gn_ref, pn_ref, nn_ref, *refs, final):
    wgb_ref, wub_ref = refs[-2:]
    out_refs = refs[:-2]

    @pl.when(pl.program_id(0) == 0)
    def _():
        wgb_ref[...] = wg_ref[...].astype(BF16)
        wub_ref[...] = wu_ref[...].astype(BF16)

    def normed(x, w_ref):
        return x * lax.rsqrt(jnp.mean(x * x, axis=-1, keepdims=True) + EPS) * w_ref[...]

    h = h_ref[...]
    gate = jax.nn.sigmoid(_dot(normed(h, gn_ref).astype(BF16), wgb_ref[...]))
    emb = normed(_dot(p_ref[...].astype(BF16), wub_ref[...]), pn_ref)
    h = h + emb * gate
    if not final:
        out_refs[0][...] = h
    out_refs[-1][...] = normed(h, nn_ref).astype(out_refs[-1].dtype)


def ple_add(h, p, w_gate, w_up, layer, gate_norm_w, post_norm_w, next_norm_w, final, tm=256):
    m, d = h.shape
    pd = p.shape[-1]
    row = lambda i: (i, 0)
    fixed = lambda i: (0, 0)
    resident = functools.partial(pl.BlockSpec, index_map=lambda i: (layer, 0, 0), pipeline_mode=pl.Buffered(1))
    out_dtypes = [F32] if final else [F32, BF16]
    return pl.pallas_call(
        functools.partial(_ple_kernel, final=final),
        grid=(m // tm,),
        in_specs=[pl.BlockSpec((tm, d), row),
                  pl.BlockSpec((None, None, tm, pd), lambda i: (layer, 0, i, 0)),
                  resident((None, d, d)), resident((None, pd, d)),
                  pl.BlockSpec((1, d), fixed), pl.BlockSpec((1, d), fixed), pl.BlockSpec((1, d), fixed)],
        out_specs=[pl.BlockSpec((tm, d), row) for _ in out_dtypes],
        out_shape=[jax.ShapeDtypeStruct((m, d), dt) for dt in out_dtypes],
        scratch_shapes=[pltpu.VMEM((d, d), BF16), pltpu.VMEM((pd, d), BF16)],
        compiler_params=_params(1),
        name="ple_add",
    )(h, p, w_gate, w_up, gate_norm_w.reshape(1, d), post_norm_w.reshape(1, d), next_norm_w.reshape(1, d))


def _ssd_kernel(z_ref, xs_ref, bc_ref, dt_ref, cwx_ref, cbx_ref, cwbc_ref, cbbc_ref, dtb_ref, alog_ref,
                dskip_ref, nw_ref, o_ref, xhalo_ref, bchalo_ref, st_ref, spread_ref):
    rows = SSD_ROWS

    @pl.when(pl.program_id(0) == 0)
    def _():
        xhalo_ref[...] = jnp.zeros(xhalo_ref.shape, BF16)
        bchalo_ref[...] = jnp.zeros(bchalo_ref.shape, BF16)
        st_ref[...] = jnp.zeros(st_ref.shape, F32)
        head = lax.broadcasted_iota(jnp.int32, spread_ref.shape, 0)
        lane = lax.broadcasted_iota(jnp.int32, spread_ref.shape, 1)
        spread_ref[...] = jnp.where(lane // SSD_HEAD_DIM == head, 1.0, 0.0).astype(BF16)

    out_row = lax.broadcasted_iota(jnp.int32, (rows, CONV_HALO + rows), 0)
    in_row = lax.broadcasted_iota(jnp.int32, (rows, CONV_HALO + rows), 1)
    shifts = [jnp.where(in_row == out_row + CONV_HALO - back, 1.0, 0.0).astype(BF16)
              for back in range(SSD_CONV - 1, 0, -1)]

    def conv_silu(halo_ref, raw_ref, w_ref, b_ref):
        raw = raw_ref[...]
        ext = jnp.concatenate([halo_ref[...], raw], axis=0)
        acc = b_ref[...] + raw.astype(F32) * w_ref[SSD_CONV - 1:SSD_CONV, :]
        for k, shift in enumerate(shifts):
            acc = acc + _dot(shift, ext) * w_ref[k:k + 1, :]
        halo_ref[...] = raw[rows - CONV_HALO:, :]
        return _silu(acc)

    xs = conv_silu(xhalo_ref, xs_ref, cwx_ref, cbx_ref)
    bc = conv_silu(bchalo_ref, bc_ref, cwbc_ref, cbbc_ref)

    is_head = lax.broadcasted_iota(jnp.int32, (rows, LANES), 1) < SSD_HEADS
    dt = jnp.where(is_head, _softplus(dt_ref[...] + dtb_ref[...]), 0.0)
    a = -jnp.exp(alog_ref[...])
    causal = _tril(rows)
    cum = _dot_f32(causal.astype(F32), dt * a)
    cum_t = cum.T
    low_half = lax.broadcasted_iota(jnp.int32, (rows, LANES), 1) < SSD_HEAD_DIM

    def per_lane(v, terms):
        out, rest = None, v
        for _ in range(terms):
            piece = rest.astype(BF16)
            rest = rest - piece.astype(F32)
            moved = _dot(piece, spread_ref[...])
            out = moved if out is None else out + moved
        return out

    dt_x = per_lane(dt, 2)
    cum_x = per_lane(cum, 3)

    pairs_per_group = SSD_GROUP_WIDTH // LANES
    for g in range(SSD_GROUPS):
        b_g = bc[:, g * SSD_STATE:(g + 1) * SSD_STATE]
        c_g = bc[:, (SSD_GROUPS + g) * SSD_STATE:(SSD_GROUPS + g + 1) * SSD_STATE].astype(BF16)
        cb = _dot_nt(c_g, b_g.astype(BF16))
        st = st_ref[g]
        y_off = _dot(c_g, st.astype(BF16))
        y_tiles, xw_tiles, dec_tiles = [], [], []
        for jj in range(pairs_per_group):
            j = g * pairs_per_group + jj
            h0 = 2 * j
            sl = slice(j * LANES, (j + 1) * LANES)
            x_p = xs[:, sl]
            cum_p = cum_x[:, sl]
            xdt = x_p * dt_x[:, sl]
            xdt_b = xdt.astype(BF16)
            halves = []
            for h in (h0, h0 + 1):
                diff = cum[:, h:h + 1] - cum_t[h:h + 1, :]
                decay = jnp.exp(jnp.where(causal, diff, -jnp.inf))
                halves.append(_dot((cb * decay).astype(BF16), xdt_b))
            y_diag = jnp.where(low_half, halves[0], halves[1])
            cum_last = cum_p[rows - 1:rows, :]
            xw_tiles.append((xdt * jnp.exp(cum_last - cum_p)).astype(BF16))
            dec_tiles.append(jnp.exp(cum_last))
            y = y_diag + y_off[:, jj * LANES:(jj + 1) * LANES] * jnp.exp(cum_p) + x_p * dskip_ref[:, sl]
            y_tiles.append(y * _silu(z_ref[:, sl].astype(F32)))
        xw = jnp.concatenate(xw_tiles, axis=1)
        st_ref[g] = st * jnp.concatenate(dec_tiles, axis=1) + _dot(b_g.T.astype(BF16), xw)
        ss = jnp.sum(y_tiles[0] * y_tiles[0], axis=-1, keepdims=True)
        for t in y_tiles[1:]:
            ss = ss + jnp.sum(t * t, axis=-1, keepdims=True)
        inv = lax.rsqrt(ss * (1.0 / SSD_GROUP_WIDTH) + EPS)
        for jj in range(pairs_per_group):
            sl = slice((g * pairs_per_group + jj) * LANES, (g * pairs_per_group + jj + 1) * LANES)
            o_ref[:, sl] = (y_tiles[jj] * inv * nw_ref[:, sl]).astype(o_ref.dtype)


def ssd_mixer(proj, dt_raw, conv_wx, conv_bx, conv_wbc, conv_bbc, dt_bias, a_log, d_skip_x, norm_w):
    t = proj.shape[0]
    rows = SSD_ROWS
    d = D_MODEL
    nbc = 2 * SSD_GROUPS * SSD_STATE
    fixed = lambda i: (0, 0)
    return pl.pallas_call(
        _ssd_kernel,
        grid=(t // rows,),
        in_specs=[pl.BlockSpec((rows, d), lambda i: (i, 0)),
                  pl.BlockSpec((rows, d), lambda i: (i, 1)),
                  pl.BlockSpec((rows, nbc), lambda i: (i, 2 * d // nbc)),
                  pl.BlockSpec((rows, LANES), lambda i: (i, 0)),
                  pl.BlockSpec((SSD_CONV, d), fixed), pl.BlockSpec((1, d), fixed),
                  pl.BlockSpec((SSD_CONV, nbc), fixed), pl.BlockSpec((1, nbc), fixed),
                  pl.BlockSpec((1, LANES), fixed), pl.BlockSpec((1, LANES), fixed),
                  pl.BlockSpec((1, d), fixed), pl.BlockSpec((1, d), fixed)],
        out_specs=pl.BlockSpec((rows, d), lambda i: (i, 0)),
        out_shape=jax.ShapeDtypeStruct((t, d), BF16),
        scratch_shapes=[pltpu.VMEM((CONV_HALO, d), BF16),
                        pltpu.VMEM((CONV_HALO, nbc), BF16),
                        pltpu.VMEM((SSD_GROUPS, SSD_STATE, SSD_GROUP_WIDTH), F32),
                        pltpu.VMEM((LANES, d), BF16)],
        compiler_params=_params(1),
        name="ssd_mixer",
    )(proj, proj, proj, dt_raw, conv_wx, conv_bx, conv_wbc, conv_bbc, dt_bias, a_log, d_skip_x, norm_w)


def _hgrn_kernel(q_ref, v_ref, g_ref, f_ref, lb_ref, nw_ref, o_ref, st_ref):
    rows = HGRN_ROWS

    @pl.when(pl.program_id(0) == 0)
    def _():
        st_ref[...] = jnp.zeros(st_ref.shape, F32)

    lb = lb_ref[...]
    causal = _tril(rows)
    tri = causal.astype(F32)
    for c in range(HGRN_CHUNKS_PER_STEP):
        r = slice(c * rows, (c + 1) * rows)
        f = lb + (1.0 - lb) * jax.nn.sigmoid(f_ref[r, :])
        k = 1.0 - f
        cum = _dot_f32(tri, jnp.log(f))
        qf = _silu(q_ref[r, :].astype(F32))
        mid = cum[rows // 2 - 1:rows // 2, :]
        last = cum[rows - 1:rows, :]
        q_rel = (qf * jnp.exp(cum - mid)).astype(BF16)
        k_rel = (k * jnp.exp(mid - cum)).astype(BF16)
        k_end = (k * jnp.exp(last - cum)).astype(BF16)
        q_dec = (qf * jnp.exp(cum)).astype(BF16)
        chunk_decay = jnp.exp(last)
        for h in range(HGRN_HEADS):
            sl = slice(h * HGRN_DIM, (h + 1) * HGRN_DIM)
            v_h = v_ref[r, sl]
            att = jnp.where(causal, _dot_nt(q_rel[:, sl], k_rel[:, sl]), 0.0)
            st = st_ref[h]
            o = _dot(att.astype(BF16), v_h) + _dot_nt(q_dec[:, sl], st.astype(BF16))
            st_ref[h] = st * chunk_decay[:, sl] + _dot_tn(v_h, k_end[:, sl])
            on = o * lax.rsqrt(jnp.mean(o * o, axis=-1, keepdims=True) + EPS) * nw_ref[:, sl]
            o_ref[r, sl] = (on * _silu(g_ref[r, sl].astype(F32))).astype(o_ref.dtype)


def hgrn_mixer(qvg, f_raw, lb, norm_w):
    t = qvg.shape[0]
    rows = HGRN_ROWS * HGRN_CHUNKS_PER_STEP
    d = D_MODEL
    fixed = lambda i: (0, 0)
    return pl.pallas_call(
        _hgrn_kernel,
        grid=(t // rows,),
        in_specs=[pl.BlockSpec((rows, d), lambda i: (i, 0)),
                  pl.BlockSpec((rows, d), lambda i: (i, 1)),
                  pl.BlockSpec((rows, d), lambda i: (i, 2)),
                  pl.BlockSpec((rows, d), lambda i: (i, 0)),
                  pl.BlockSpec((1, d), fixed), pl.BlockSpec((1, d), fixed)],
        out_specs=pl.BlockSpec((rows, d), lambda i: (i, 0)),
        out_shape=jax.ShapeDtypeStruct((t, d), BF16),
        scratch_shapes=[pltpu.VMEM((HGRN_HEADS, HGRN_DIM, HGRN_DIM), F32)],
        compiler_params=_params(1),
        name="hgrn_mixer",
    )(qvg, qvg, qvg, f_raw, lb, norm_w)


def _logf_cumsum_kernel(f_ref, b_ref, o_ref, ot_ref, carry_ref):
    rows = f_ref.shape[0]

    @pl.when(pl.program_id(0) == 0)
    def _():
        carry_ref[...] = jnp.zeros(carry_ref.shape, F32)

    log_f = -_softplus(-(f_ref[...] + b_ref[...]))
    c = _dot_f32(_tril(rows).astype(F32), log_f) + carry_ref[...]
    carry_ref[...] = c[rows - 1:rows, :]
    c2 = c * LOG2E
    o_ref[...] = c2
    ot_ref[...] = c2.T


def logf_cumsum(f_raw, b_f, rows=512):
    t = f_raw.shape[0]
    return pl.pallas_call(
        _logf_cumsum_kernel,
        grid=(t // rows,),
        in_specs=[pl.BlockSpec((rows, LANES), lambda i: (i, 0)), pl.BlockSpec((1, LANES), lambda i: (0, 0))],
        out_specs=[pl.BlockSpec((rows, LANES), lambda i: (i, 0)), pl.BlockSpec((LANES, rows), lambda i: (0, i))],
        out_shape=[jax.ShapeDtypeStruct((t, LANES), F32), jax.ShapeDtypeStruct((LANES, t), F32)],
        scratch_shapes=[pltpu.VMEM((1, LANES), F32)],
        compiler_params=_params(1),
        name="logf_cumsum",
    )(f_raw, b_f)


def _fox_kernel(q_ref, k_ref, v_ref, dkc_ref, dqr_ref, o_ref):
    tq, tk = FOX_Q_BLOCK, FOX_K_BLOCK
    half = tq // 2
    hp = pl.program_id(0)
    qi = pl.program_id(1)
    q_start = pl.multiple_of(qi * tq, tq)
    heads = []
    for hh in range(FOX_PACK):
        sl = slice(hh * FOX_DIM, (hh + 1) * FOX_DIM)
        heads.append((hh, sl, q_ref[:, sl], dqr_ref[hh, :, pl.ds(q_start, tq)]))

    def step(start, width, lo, carry, diagonal):
        out = []
        lane = lax.broadcasted_iota(jnp.int32, (width, LANES), 1)
        dk_all = dkc_ref[pl.ds(start, width), :]
        for (hh, sl, q, dq), (m, l, acc) in zip(heads, carry):
            k_b = k_ref[pl.ds(start, width), sl]
            v_b = v_ref[pl.ds(start, width), sl]
            dk = jnp.sum(jnp.where(lane == FOX_F_LANE0 + hp * FOX_PACK + hh, dk_all, 0.0), axis=-1, keepdims=True)
            s = _dot_nt(k_b, q[lo:]) - dk
            if diagonal:
                key = lax.broadcasted_iota(jnp.int32, s.shape, 0)
                qry = lax.broadcasted_iota(jnp.int32, s.shape, 1)
                s = jnp.where(qry >= key, s, -jnp.inf)
            m_new = jnp.maximum(m, jnp.max(s, axis=0, keepdims=True) + dq[:, lo:])
            alpha = jnp.exp2(m - m_new)
            p = jnp.exp2(s - (m_new - dq[:, lo:]))
            l = alpha * l + jnp.sum(p, axis=0, keepdims=True)
            acc = alpha * acc + _dot_tn(v_b, p.astype(BF16))
            out.append((m_new, l, acc))
        return tuple(out)

    init = tuple((jnp.full((1, tq), -jnp.inf, F32), jnp.zeros((1, tq), F32), jnp.zeros((FOX_DIM, tq), F32))
                 for _ in heads)
    n_wide = lax.shift_right_logical(qi, 1)
    carry = lax.fori_loop(0, n_wide, lambda ki, c: step(pl.multiple_of(ki * tk, tk), tk, 0, c, False), init)
    odd_start = pl.multiple_of(n_wide * tk, tk)
    carry = lax.fori_loop(0, qi & 1, lambda _, c: step(odd_start, tq, 0, c, False), carry)
    carry = step(q_start, half, 0, carry, True)
    hi = step(pl.multiple_of(q_start + half, half), half, half,
              tuple((m[:, half:], l[:, half:], acc[:, half:]) for m, l, acc in carry), True)
    for (hh, sl, _, _), (m, l, acc), (m_hi, l_hi, acc_hi) in zip(heads, carry, hi):
        l = jnp.concatenate([l[:, :half], l_hi], axis=1)
        acc = jnp.concatenate([acc[:, :half], acc_hi], axis=1)
        o_ref[:, sl] = (acc / l).T.astype(o_ref.dtype)


def fox_attention(qkv, dcum, dcum_t):
    t = qkv.shape[0]
    tq = FOX_Q_BLOCK
    width = FOX_PACK * FOX_DIM
    groups = FOX_HEADS // FOX_PACK
    return pl.pallas_call(
        _fox_kernel,
        grid=(groups, t // tq),
        in_specs=[pl.BlockSpec((tq, width), lambda h, i: (i, h)),
                  pl.BlockSpec((t, width), lambda h, i: (0, groups + h)),
                  pl.BlockSpec((t, width), lambda h, i: (0, 2 * groups + h)),
                  pl.BlockSpec((t, LANES), lambda h, i: (0, 0)),
                  pl.BlockSpec((FOX_PACK, 1, t), lambda h, i: (h, 0, 0))],
        out_specs=pl.BlockSpec((tq, width), lambda h, i: (i, h)),
        out_shape=jax.ShapeDtypeStruct((t, FOX_HEADS * FOX_DIM), BF16),
        compiler_params=_params(2),
        name="fox_attention",
    )(qkv, qkv, qkv, dcum, dcum_t)


def _swiglu_half(h, hn, w_in, w_out, layer):
    n_main = (D_FF // FFN_TILE) * FFN_TILE
    act_main = ffn_in(hn, w_in, layer, 0, n_main, FFN_TILE)
    act_tail = ffn_in(hn, w_in, layer, n_main, D_FF - n_main, D_FF - n_main)
    return matmul_residual(h, w_out, layer, [(act_main, 0), (act_tail, n_main)], scale=0.5)


def _ssd_hgrn_layer(h, hn, w_in, layer, conv_w, conv_b, dt_bias, a_log, d_skip, ssd_norm_w, lb, hgrn_norm_w, w_out):
    d = D_MODEL
    nb = SSD_GROUPS * SSD_STATE
    o_dt = 2 * d + 2 * nb
    o_q = o_dt + SSD_HEADS
    tile = 1024 // SUBLANES
    wt = jnp.swapaxes(w_in, 1, 2)
    proj = matmul_nt(hn, wt, layer, BF16, o_dt, lambda j: j * tile)
    dt_raw = matmul_nt(hn, wt, layer, F32, LANES, lambda j: o_dt // SUBLANES, tn=LANES)
    first = o_q // SUBLANES
    qvg = matmul_nt(hn, wt, layer, BF16, 3 * d, lambda j: first + (j + jnp.where(j >= 2, 2, 0)) * tile)
    f_raw = matmul_nt(hn, wt, layer, F32, d, lambda j: first + (j + 2) * tile)
    pad_heads = lambda v: jnp.pad(v, (0, LANES - SSD_HEADS)).reshape(1, LANES)
    y_a = ssd_mixer(proj, dt_raw, conv_w[:, :d], conv_b[:d].reshape(1, d), conv_w[:, d:],
                    conv_b[d:].reshape(1, 2 * nb), pad_heads(dt_bias), pad_heads(a_log),
                    jnp.repeat(d_skip, SSD_HEAD_DIM).reshape(1, d), ssd_norm_w.reshape(1, d))
    y_b = hgrn_mixer(qvg, f_raw, lb.reshape(1, d), hgrn_norm_w.reshape(1, d))
    return matmul_residual(h, w_out, layer, [(y_a, 0), (y_b, d)], tm=1024)


def _fox_layer(h, hn, w_in, layer, b_f, w_out):
    d = D_MODEL
    t = h.shape[0]
    tile = 1024 // SUBLANES
    wt = jnp.swapaxes(w_in, 1, 2)
    qkv = matmul_nt(hn, wt, layer, BF16, 3 * d, lambda j: j * tile, scaled_blocks=d // 1024,
                    scale=LOG2E * FOX_DIM ** -0.5)
    f_raw = matmul_nt(hn, wt, layer, F32, LANES, lambda j: (3 * d + FOX_HEADS - LANES) // SUBLANES, tn=LANES)
    b_pad = jnp.pad(b_f, (FOX_F_LANE0, 0)).reshape(1, LANES)
    dcum, dcum_t = logf_cumsum(f_raw, b_pad)
    o = fox_attention(qkv, dcum, dcum_t[FOX_F_LANE0:].reshape(FOX_HEADS, 1, t))
    return matmul_residual(h, w_out, layer, [(o, 0)], tm=1024)


def kernel(x, p, ffn1_norm, ffn1_w_in, ffn1_w_out, mix_norm, ab_w_in, ssd_conv_w, ssd_conv_b, ssd_dt_bias,
           ssd_a_log, ssd_d, ssd_norm, hgrn_lb_logits, hgrn_norm, ab_w_out, fox_w_in, fox_b_f, fox_w_out,
           ffn2_norm, ffn2_w_in, ffn2_w_out, ple_gate_norm, ple_w_gate, ple_w_up, ple_norm, final_norm):
    bsz, t, d = x.shape
    depth = p.shape[0]
    assert bsz == 1 and d == D_MODEL
    lb_all = jnp.cumsum(jax.nn.softmax(hgrn_lb_logits.astype(F32), axis=0), axis=0)
    h = x.reshape(t, d)
    hn = rmsnorm(h, ffn1_norm[0], BF16)
    for i in range(depth):
        j = i // 2
        h = _swiglu_half(h, hn, ffn1_w_in, ffn1_w_out, i)
        hn = rmsnorm(h, mix_norm[i], BF16)
        if i % 2 == 0:
            h = _ssd_hgrn_layer(h, hn, ab_w_in, j, ssd_conv_w[j], ssd_conv_b[j], ssd_dt_bias[j], ssd_a_log[j],
                                ssd_d[j], ssd_norm[j], lb_all[i], hgrn_norm[j], ab_w_out)
        else:
            h = _fox_layer(h, hn, fox_w_in, j, fox_b_f[j], fox_w_out)
        hn = rmsnorm(h, ffn2_norm[i], BF16)
        h = _swiglu_half(h, hn, ffn2_w_in, ffn2_w_out, i)
        final = i == depth - 1
        next_w = final_norm if final else ffn1_norm[i + 1]
        outs = ple_add(h, p, ple_w_gate, ple_w_up, i, ple_gate_norm[i], ple_norm[i], next_w, final)
        if final:
            return outs[0].reshape(bsz, t, d)
        h, hn = outs
```

```python
import functools
import math

import jax
import jax.numpy as jnp
from jax import lax
from jax.experimental import pallas as pl
from jax.experimental.pallas import tpu as pltpu

F32 = jnp.float32
BF16 = jnp.bfloat16
EPS = 1e-6

D_MODEL = 2048
D_FF = 5504
SSD_HEADS = 32
SSD_HEAD_DIM = 64
SSD_GROUPS = 4
SSD_STATE = 128
SSD_GROUP_WIDTH = 512
SSD_CONV = 4
HGRN_HEADS = 16
HGRN_DIM = 128
FOX_HEADS = 16
FOX_DIM = 128
LANES = 128
SUBLANES = 8
CONV_HALO = 16

SSD_ROWS = 128
HGRN_ROWS = 64
HGRN_CHUNKS_PER_STEP = 4
FOX_Q_BLOCK = 1024
FOX_K_BLOCK = 2048
FOX_PACK = 2
FOX_F_LANE0 = LANES - FOX_HEADS
FFN_TILE = 512
VMEM_LIMIT = 56 * 1024 * 1024
LOG2E = math.log2(math.e)


def _params(n_axes, vmem=VMEM_LIMIT):
    return pltpu.CompilerParams(dimension_semantics=("arbitrary",) * n_axes, vmem_limit_bytes=vmem)


def _silu(x):
    return x * jax.nn.sigmoid(x)


def _softplus(x):
    return jnp.maximum(x, 0.0) + jnp.log(1.0 + jnp.exp(-jnp.abs(x)))


def _dot(a, b):
    return jnp.dot(a, b, preferred_element_type=F32)


def _dot_nt(a, b):
    return lax.dot_general(a, b, (((1,), (1,)), ((), ())), preferred_element_type=F32)


def _dot_tn(a, b):
    return lax.dot_general(a, b, (((0,), (0,)), ((), ())), preferred_element_type=F32)


def _dot_f32(a, b):
    return jnp.dot(a, b, preferred_element_type=F32, precision=lax.Precision.HIGHEST)


def _tril(n):
    r = lax.broadcasted_iota(jnp.int32, (n, n), 0)
    c = lax.broadcasted_iota(jnp.int32, (n, n), 1)
    return r >= c


def _weight_window(layer, rows, width, row_tile, col_tile):
    return pl.BlockSpec((pl.Element(1), pl.Element(rows), pl.Element(width)),
                        lambda j, i: (layer, row_tile(j) * SUBLANES, col_tile(j) * LANES))


def _cast_weights_once(pairs):
    @pl.when(pl.program_id(1) == 0)
    def _():
        for w_ref, wb_ref in pairs:
            wb_ref[...] = w_ref[0].astype(BF16)


def _normed_rows(x_ref, nw_ref, hn_ref):
    x = x_ref[...]
    xn = (x * lax.rsqrt(jnp.mean(x * x, axis=-1, keepdims=True) + EPS) * nw_ref[...]).astype(BF16)
    hn_ref[...] = xn
    return xn


def _mm_nt_kernel(x_ref, w_ref, *refs, scaled_blocks, scale, normalize):
    wb_ref = refs[-1]
    _cast_weights_once([(w_ref, wb_ref)])
    if normalize:
        nw_ref, o_ref, hn_ref = refs[:3]
        x = _normed_rows(x_ref, nw_ref, hn_ref)
    else:
        o_ref = refs[0]
        x = x_ref[...]
    acc = _dot_nt(x, wb_ref[...])
    if scaled_blocks:
        acc = acc * jnp.where(pl.program_id(0) < scaled_blocks, scale, 1.0)
    o_ref[...] = acc.astype(o_ref.dtype)


def matmul_nt(x, wt, layer, out_dtype, n_out, row_tile, tn=1024, tm=1024, scaled_blocks=0, scale=1.0, norm_w=None):
    m, k = x.shape
    normalize = norm_w is not None
    assert not normalize or n_out == tn
    in_specs = [pl.BlockSpec((tm, k), lambda j, i: (i, 0)), _weight_window(layer, tn, k, row_tile, lambda j: 0)]
    out_specs = [pl.BlockSpec((tm, tn), lambda j, i: (i, j))]
    out_shape = [jax.ShapeDtypeStruct((m, n_out), out_dtype)]
    args = [x, wt]
    if normalize:
        in_specs.append(pl.BlockSpec((1, k), lambda j, i: (0, 0)))
        out_specs.append(pl.BlockSpec((tm, k), lambda j, i: (i, 0)))
        out_shape.append(jax.ShapeDtypeStruct((m, k), BF16))
        args.append(norm_w.reshape(1, k))
    outs = pl.pallas_call(
        functools.partial(_mm_nt_kernel, scaled_blocks=scaled_blocks, scale=scale, normalize=normalize),
        grid=(n_out // tn, m // tm),
        in_specs=in_specs,
        out_specs=out_specs,
        out_shape=out_shape,
        scratch_shapes=[pltpu.VMEM((tn, k), BF16)],
        compiler_params=_params(2),
        name="matmul_nt",
    )(*args)
    return outs if normalize else outs[0]


def _ffn_in_kernel(x_ref, wg_ref, wu_ref, *refs, normalize):
    wgb_ref, wub_ref = refs[-2:]
    _cast_weights_once([(wg_ref, wgb_ref), (wu_ref, wub_ref)])
    if normalize:
        nw_ref, o_ref, hn_ref = refs[:3]
        x = _normed_rows(x_ref, nw_ref, hn_ref)
    else:
        o_ref = refs[0]
        x = x_ref[...]
    gate = _dot(x, wgb_ref[...])
    up = _dot(x, wub_ref[...])
    o_ref[...] = (_silu(gate) * up).astype(o_ref.dtype)


def ffn_in(x, w_in, layer, col0, n_out, tn, tm=1024, norm_w=None):
    m, k = x.shape
    normalize = norm_w is not None
    assert not normalize or n_out == tn
    blocks_per_tile = tn // LANES
    gate_col = lambda j: col0 // LANES + j * blocks_per_tile
    up_col = lambda j: (D_FF + col0) // LANES + j * blocks_per_tile
    in_specs = [pl.BlockSpec((tm, k), lambda j, i: (i, 0)),
                _weight_window(layer, k, tn, lambda j: 0, gate_col),
                _weight_window(layer, k, tn, lambda j: 0, up_col)]
    out_specs = [pl.BlockSpec((tm, tn), lambda j, i: (i, j))]
    out_shape = [jax.ShapeDtypeStruct((m, n_out), BF16)]
    args = [x, w_in, w_in]
    if normalize:
        in_specs.append(pl.BlockSpec((1, k), lambda j, i: (0, 0)))
        out_specs.append(pl.BlockSpec((tm, k), lambda j, i: (i, 0)))
        out_shape.append(jax.ShapeDtypeStruct((m, k), BF16))
        args.append(norm_w.reshape(1, k))
    outs = pl.pallas_call(
        functools.partial(_ffn_in_kernel, normalize=normalize),
        grid=(n_out // tn, m // tm),
        in_specs=in_specs,
        out_specs=out_specs,
        out_shape=out_shape,
        scratch_shapes=[pltpu.VMEM((k, tn), BF16), pltpu.VMEM((k, tn), BF16)],
        compiler_params=_params(2),
        name="ffn_in",
    )(*args)
    return outs if normalize else outs[0]


def _mm_resid_kernel(*refs, n_pairs, scale):
    h_ref = refs[0]
    a_refs = refs[1:1 + n_pairs]
    w_refs = refs[1 + n_pairs:1 + 2 * n_pairs]
    o_ref = refs[1 + 2 * n_pairs]
    wb_refs = refs[2 + 2 * n_pairs:]
    _cast_weights_once(list(zip(w_refs, wb_refs)))
    acc = _dot(a_refs[0][...], wb_refs[0][...])
    for a_ref, wb_ref in zip(a_refs[1:], wb_refs[1:]):
        acc = acc + _dot(a_ref[...], wb_ref[...])
    if scale != 1.0:
        acc = scale * acc
    o_ref[...] = h_ref[...] + acc


def matmul_residual(h, w, layer, pieces, scale=1.0, tm=512, tn=512):
    m, n = h.shape
    blocks_per_tile = tn // LANES
    a_specs = [pl.BlockSpec((tm, a.shape[1]), lambda j, i: (i, 0)) for a, _ in pieces]
    w_specs = [_weight_window(layer, a.shape[1], tn, lambda j, r=row0 // SUBLANES: r, lambda j: j * blocks_per_tile)
               for a, row0 in pieces]
    return pl.pallas_call(
        functools.partial(_mm_resid_kernel, n_pairs=len(pieces), scale=scale),
        grid=(n // tn, m // tm),
        in_specs=[pl.BlockSpec((tm, tn), lambda j, i: (i, j))] + a_specs + w_specs,
        out_specs=pl.BlockSpec((tm, tn), lambda j, i: (i, j)),
        out_shape=jax.ShapeDtypeStruct((m, n), F32),
        scratch_shapes=[pltpu.VMEM((a.shape[1], tn), BF16) for a, _ in pieces],
        compiler_params=_params(2),
        name="matmul_residual",
    )(h, *[a for a, _ in pieces], *[w] * len(pieces))


def _ple_kernel(h_ref, p_ref, wg_ref, wu_ref, gn_ref, pn_ref, nn_ref, *refs, final):
    wgb_ref, wub_ref = refs[-2:]
    out_refs = refs[:-2]

    @pl.when(pl.program_id(0) == 0)
    def _():
        wgb_ref[...] = wg_ref[...].astype(BF16)
        wub_ref[...] = wu_ref[...].astype(BF16)

    def normed(x, w_ref):
        return x * lax.rsqrt(jnp.mean(x * x, axis=-1, keepdims=True) + EPS) * w_ref[...]

    h = h_ref[...]
    gate = jax.nn.sigmoid(_dot(normed(h, gn_ref).astype(BF16), wgb_ref[...]))
    emb = normed(_dot(p_ref[...].astype(BF16), wub_ref[...]), pn_ref)
    h = h + emb * gate
    if not final:
        out_refs[0][...] = h
    out_refs[-1][...] = normed(h, nn_ref).astype(out_refs[-1].dtype)


def ple_add(h, p, w_gate, w_up, layer, gate_norm_w, post_norm_w, next_norm_w, final, tm=256):
    m, d = h.shape
    pd = p.shape[-1]
    row = lambda i: (i, 0)
    fixed = lambda i: (0, 0)
    resident = functools.partial(pl.BlockSpec, index_map=lambda i: (layer, 0, 0), pipeline_mode=pl.Buffered(1))
    out_dtypes = [F32] if final else [F32, BF16]
    return pl.pallas_call(
        functools.partial(_ple_kernel, final=final),
        grid=(m // tm,),
        in_specs=[pl.BlockSpec((tm, d), row),
                  pl.BlockSpec((None, None, tm, pd), lambda i: (layer, 0, i, 0)),
                  resident((None, d, d)), resident((None, pd, d)),
                  pl.BlockSpec((1, d), fixed), pl.BlockSpec((1, d), fixed), pl.BlockSpec((1, d), fixed)],
        out_specs=[pl.BlockSpec((tm, d), row) for _ in out_dtypes],
        out_shape=[jax.ShapeDtypeStruct((m, d), dt) for dt in out_dtypes],
        scratch_shapes=[pltpu.VMEM((d, d), BF16), pltpu.VMEM((pd, d), BF16)],
        compiler_params=_params(1),
        name="ple_add",
    )(h, p, w_gate, w_up, gate_norm_w.reshape(1, d), post_norm_w.reshape(1, d), next_norm_w.reshape(1, d))


def _ssd_kernel(z_ref, xs_ref, bc_ref, dt_ref, cwx_ref, cbx_ref, cwbc_ref, cbbc_ref, dtb_ref, alog_ref,
                dskip_ref, nw_ref, o_ref, xhalo_ref, bchalo_ref, st_ref, spread_ref):
    rows = SSD_ROWS

    @pl.when(pl.program_id(0) == 0)
    def _():
        xhalo_ref[...] = jnp.zeros(xhalo_ref.shape, BF16)
        bchalo_ref[...] = jnp.zeros(bchalo_ref.shape, BF16)
        st_ref[...] = jnp.zeros(st_ref.shape, F32)
        head = lax.broadcasted_iota(jnp.int32, spread_ref.shape, 0)
        lane = lax.broadcasted_iota(jnp.int32, spread_ref.shape, 1)
        spread_ref[...] = jnp.where(lane // SSD_HEAD_DIM == head, 1.0, 0.0).astype(BF16)

    out_row = lax.broadcasted_iota(jnp.int32, (rows, CONV_HALO + rows), 0)
    in_row = lax.broadcasted_iota(jnp.int32, (rows, CONV_HALO + rows), 1)
    shifts = [jnp.where(in_row == out_row + CONV_HALO - back, 1.0, 0.0).astype(BF16)
              for back in range(SSD_CONV - 1, 0, -1)]

    def conv_silu(halo_ref, raw_ref, w_ref, b_ref):
        raw = raw_ref[...]
        ext = jnp.concatenate([halo_ref[...], raw], axis=0)
        acc = b_ref[...] + raw.astype(F32) * w_ref[SSD_CONV - 1:SSD_CONV, :]
        for k, shift in enumerate(shifts):
            acc = acc + _dot(shift, ext) * w_ref[k:k + 1, :]
        halo_ref[...] = raw[rows - CONV_HALO:, :]
        return _silu(acc)

    xs = conv_silu(xhalo_ref, xs_ref, cwx_ref, cbx_ref)
    bc = conv_silu(bchalo_ref, bc_ref, cwbc_ref, cbbc_ref)

    is_head = lax.broadcasted_iota(jnp.int32, (rows, LANES), 1) < SSD_HEADS
    dt = jnp.where(is_head, _softplus(dt_ref[...] + dtb_ref[...]), 0.0)
    a = -jnp.exp(alog_ref[...])
    causal = _tril(rows)
    cum = _dot_f32(causal.astype(F32), dt * a)
    cum_t = cum.T
    low_half = lax.broadcasted_iota(jnp.int32, (rows, LANES), 1) < SSD_HEAD_DIM

    def per_lane(v, terms):
        out, rest = None, v
        for _ in range(terms):
            piece = rest.astype(BF16)
            rest = rest - piece.astype(F32)
            moved = _dot(piece, spread_ref[...])
            out = moved if out is None else out + moved
        return out

    dt_x = per_lane(dt, 2)
    cum_x = per_lane(cum, 3)

    pairs_per_group = SSD_GROUP_WIDTH // LANES
    for g in range(SSD_GROUPS):
        b_g = bc[:, g * SSD_STATE:(g + 1) * SSD_STATE]
        c_g = bc[:, (SSD_GROUPS + g) * SSD_STATE:(SSD_GROUPS + g + 1) * SSD_STATE].astype(BF16)
        cb = _dot_nt(c_g, b_g.astype(BF16))
        st = st_ref[g]
        y_off = _dot(c_g, st.astype(BF16))
        y_tiles, xw_tiles, dec_tiles = [], [], []
        for jj in range(pairs_per_group):
            j = g * pairs_per_group + jj
            h0 = 2 * j
            sl = slice(j * LANES, (j + 1) * LANES)
            x_p = xs[:, sl]
            cum_p = cum_x[:, sl]
            xdt = x_p * dt_x[:, sl]
            xdt_b = xdt.astype(BF16)
            halves = []
            for h in (h0, h0 + 1):
                diff = cum[:, h:h + 1] - cum_t[h:h + 1, :]
                decay = jnp.exp(jnp.where(causal, diff, -jnp.inf))
                halves.append(_dot((cb * decay).astype(BF16), xdt_b))
            y_diag = jnp.where(low_half, halves[0], halves[1])
            cum_last = cum_p[rows - 1:rows, :]
            xw_tiles.append((xdt * jnp.exp(cum_last - cum_p)).astype(BF16))
            dec_tiles.append(jnp.exp(cum_last))
            y = y_diag + y_off[:, jj * LANES:(jj + 1) * LANES] * jnp.exp(cum_p) + x_p * dskip_ref[:, sl]
            y_tiles.append(y * _silu(z_ref[:, sl].astype(F32)))
        xw = jnp.concatenate(xw_tiles, axis=1)
        st_ref[g] = st * jnp.concatenate(dec_tiles, axis=1) + _dot(b_g.T.astype(BF16), xw)
        ss = jnp.sum(y_tiles[0] * y_tiles[0], axis=-1, keepdims=True)
        for t in y_tiles[1:]:
            ss = ss + jnp.sum(t * t, axis=-1, keepdims=True)
        inv = lax.rsqrt(ss * (1.0 / SSD_GROUP_WIDTH) + EPS)
        for jj in range(pairs_per_group):
            sl = slice((g * pairs_per_group + jj) * LANES, (g * pairs_per_group + jj + 1) * LANES)
            o_ref[:, sl] = (y_tiles[jj] * inv * nw_ref[:, sl]).astype(o_ref.dtype)


def ssd_mixer(proj, dt_raw, conv_wx, conv_bx, conv_wbc, conv_bbc, dt_bias, a_log, d_skip_x, norm_w):
    t = proj.shape[0]
    rows = SSD_ROWS
    d = D_MODEL
    nbc = 2 * SSD_GROUPS * SSD_STATE
    fixed = lambda i: (0, 0)
    return pl.pallas_call(
        _ssd_kernel,
        grid=(t // rows,),
        in_specs=[pl.BlockSpec((rows, d), lambda i: (i, 0)),
                  pl.BlockSpec((rows, d), lambda i: (i, 1)),
                  pl.BlockSpec((rows, nbc), lambda i: (i, 2 * d // nbc)),
                  pl.BlockSpec((rows, LANES), lambda i: (i, 0)),
                  pl.BlockSpec((SSD_CONV, d), fixed), pl.BlockSpec((1, d), fixed),
                  pl.BlockSpec((SSD_CONV, nbc), fixed), pl.BlockSpec((1, nbc), fixed),
                  pl.BlockSpec((1, LANES), fixed), pl.BlockSpec((1, LANES), fixed),
                  pl.BlockSpec((1, d), fixed), pl.BlockSpec((1, d), fixed)],
        out_specs=pl.BlockSpec((rows, d), lambda i: (i, 0)),
        out_shape=jax.ShapeDtypeStruct((t, d), BF16),
        scratch_shapes=[pltpu.VMEM((CONV_HALO, d), BF16),
                        pltpu.VMEM((CONV_HALO, nbc), BF16),
                        pltpu.VMEM((SSD_GROUPS, SSD_STATE, SSD_GROUP_WIDTH), F32),
                        pltpu.VMEM((LANES, d), BF16)],
        compiler_params=_params(1),
        name="ssd_mixer",
    )(proj, proj, proj, dt_raw, conv_wx, conv_bx, conv_wbc, conv_bbc, dt_bias, a_log, d_skip_x, norm_w)


def _hgrn_kernel(q_ref, v_ref, g_ref, f_ref, lb_ref, nw_ref, o_ref, st_ref):
    rows = HGRN_ROWS

    @pl.when(pl.program_id(0) == 0)
    def _():
        st_ref[...] = jnp.zeros(st_ref.shape, F32)

    lb = lb_ref[...]
    causal = _tril(rows)
    tri = causal.astype(F32)
    for c in range(HGRN_CHUNKS_PER_STEP):
        r = slice(c * rows, (c + 1) * rows)
        f = lb + (1.0 - lb) * jax.nn.sigmoid(f_ref[r, :])
        k = 1.0 - f
        cum = _dot_f32(tri, jnp.log(f))
        qf = _silu(q_ref[r, :].astype(F32))
        mid = cum[rows // 2 - 1:rows // 2, :]
        last = cum[rows - 1:rows, :]
        q_rel = (qf * jnp.exp(cum - mid)).astype(BF16)
        k_rel = (k * jnp.exp(mid - cum)).astype(BF16)
        k_end = (k * jnp.exp(last - cum)).astype(BF16)
        q_dec = (qf * jnp.exp(cum)).astype(BF16)
        chunk_decay = jnp.exp(last)
        for h in range(HGRN_HEADS):
            sl = slice(h * HGRN_DIM, (h + 1) * HGRN_DIM)
            v_h = v_ref[r, sl]
            att = jnp.where(causal, _dot_nt(q_rel[:, sl], k_rel[:, sl]), 0.0)
            st = st_ref[h]
            o = _dot(att.astype(BF16), v_h) + _dot_nt(q_dec[:, sl], st.astype(BF16))
            st_ref[h] = st * chunk_decay[:, sl] + _dot_tn(v_h, k_end[:, sl])
            on = o * lax.rsqrt(jnp.mean(o * o, axis=-1, keepdims=True) + EPS) * nw_ref[:, sl]
            o_ref[r, sl] = (on * _silu(g_ref[r, sl].astype(F32))).astype(o_ref.dtype)


def hgrn_mixer(qvg, f_raw, lb, norm_w):
    t = qvg.shape[0]
    rows = HGRN_ROWS * HGRN_CHUNKS_PER_STEP
    d = D_MODEL
    fixed = lambda i: (0, 0)
    return pl.pallas_call(
        _hgrn_kernel,
        grid=(t // rows,),
        in_specs=[pl.BlockSpec((rows, d), lambda i: (i, 0)),
                  pl.BlockSpec((rows, d), lambda i: (i, 1)),
                  pl.BlockSpec((rows, d), lambda i: (i, 2)),
                  pl.BlockSpec((rows, d), lambda i: (i, 0)),
                  pl.BlockSpec((1, d), fixed), pl.BlockSpec((1, d), fixed)],
        out_specs=pl.BlockSpec((rows, d), lambda i: (i, 0)),
        out_shape=jax.ShapeDtypeStruct((t, d), BF16),
        scratch_shapes=[pltpu.VMEM((HGRN_HEADS, HGRN_DIM, HGRN_DIM), F32)],
        compiler_params=_params(1),
        name="hgrn_mixer",
    )(qvg, qvg, qvg, f_raw, lb, norm_w)


def _logf_cumsum_kernel(f_ref, b_ref, o_ref, ot_ref, carry_ref):
    rows = f_ref.shape[0]

    @pl.when(pl.program_id(0) == 0)
    def _():
        carry_ref[...] = jnp.zeros(carry_ref.shape, F32)

    log_f = -_softplus(-(f_ref[...] + b_ref[...]))
    c = _dot_f32(_tril(rows).astype(F32), log_f) + carry_ref[...]
    carry_ref[...] = c[rows - 1:rows, :]
    c2 = c * LOG2E
    o_ref[...] = c2
    ot_ref[...] = c2.T


def logf_cumsum(f_raw, b_f, rows=512):
    t = f_raw.shape[0]
    return pl.pallas_call(
        _logf_cumsum_kernel,
        grid=(t // rows,),
        in_specs=[pl.BlockSpec((rows, LANES), lambda i: (i, 0)), pl.BlockSpec((1, LANES), lambda i: (0, 0))],
        out_specs=[pl.BlockSpec((rows, LANES), lambda i: (i, 0)), pl.BlockSpec((LANES, rows), lambda i: (0, i))],
        out_shape=[jax.ShapeDtypeStruct((t, LANES), F32), jax.ShapeDtypeStruct((LANES, t), F32)],
        scratch_shapes=[pltpu.VMEM((1, LANES), F32)],
        compiler_params=_params(1),
        name="logf_cumsum",
    )(f_raw, b_f)


def _fox_kernel(q_ref, k_ref, v_ref, dkc_ref, dqr_ref, o_ref):
    tq, tk = FOX_Q_BLOCK, FOX_K_BLOCK
    half = tq // 2
    hp = pl.program_id(0)
    qi = pl.program_id(1)
    q_start = pl.multiple_of(qi * tq, tq)
    heads = []
    for hh in range(FOX_PACK):
        sl = slice(hh * FOX_DIM, (hh + 1) * FOX_DIM)
        heads.append((hh, sl, q_ref[:, sl], dqr_ref[hh, :, pl.ds(q_start, tq)]))

    def step(start, width, lo, carry, diagonal):
        out = []
        lane = lax.broadcasted_iota(jnp.int32, (width, LANES), 1)
        dk_all = dkc_ref[pl.ds(start, width), :]
        for (hh, sl, q, dq), (m, l, acc) in zip(heads, carry):
            k_b = k_ref[pl.ds(start, width), sl]
            v_b = v_ref[pl.ds(start, width), sl]
            dk = jnp.sum(jnp.where(lane == FOX_F_LANE0 + hp * FOX_PACK + hh, dk_all, 0.0), axis=-1, keepdims=True)
            s = _dot_nt(k_b, q[lo:]) - dk
            if diagonal:
                key = lax.broadcasted_iota(jnp.int32, s.shape, 0)
                qry = lax.broadcasted_iota(jnp.int32, s.shape, 1)
                s = jnp.where(qry >= key, s, -jnp.inf)
            m_new = jnp.maximum(m, jnp.max(s, axis=0, keepdims=True) + dq[:, lo:])
            alpha = jnp.exp2(m - m_new)
            p = jnp.exp2(s - (m_new - dq[:, lo:]))
            l = alpha * l + jnp.sum(p, axis=0, keepdims=True)
            acc = alpha * acc + _dot_tn(v_b, p.astype(BF16))
            out.append((m_new, l, acc))
        return tuple(out)

    init = tuple((jnp.full((1, tq), -jnp.inf, F32), jnp.zeros((1, tq), F32), jnp.zeros((FOX_DIM, tq), F32))
                 for _ in heads)
    n_wide = lax.shift_right_logical(qi, 1)
    carry = lax.fori_loop(0, n_wide, lambda ki, c: step(pl.multiple_of(ki * tk, tk), tk, 0, c, False), init)
    odd_start = pl.multiple_of(n_wide * tk, tk)
    carry = lax.fori_loop(0, qi & 1, lambda _, c: step(odd_start, tq, 0, c, False), carry)
    carry = step(q_start, half, 0, carry, True)
    hi = step(pl.multiple_of(q_start + half, half), half, half,
              tuple((m[:, half:], l[:, half:], acc[:, half:]) for m, l, acc in carry), True)
    for (hh, sl, _, _), (m, l, acc), (m_hi, l_hi, acc_hi) in zip(heads, carry, hi):
        l = jnp.concatenate([l[:, :half], l_hi], axis=1)
        acc = jnp.concatenate([acc[:, :half], acc_hi], axis=1)
        o_ref[:, sl] = (acc / l).T.astype(o_ref.dtype)


def fox_attention(qkv, dcum, dcum_t):
    t = qkv.shape[0]
    tq = FOX_Q_BLOCK
    width = FOX_PACK * FOX_DIM
    groups = FOX_HEADS // FOX_PACK
    return pl.pallas_call(
        _fox_kernel,
        grid=(groups, t // tq),
        in_specs=[pl.BlockSpec((tq, width), lambda h, i: (i, h)),
                  pl.BlockSpec((t, width), lambda h, i: (0, groups + h)),
                  pl.BlockSpec((t, width), lambda h, i: (0, 2 * groups + h)),
                  pl.BlockSpec((t, LANES), lambda h, i: (0, 0)),
                  pl.BlockSpec((FOX_PACK, 1, t), lambda h, i: (h, 0, 0))],
        out_specs=pl.BlockSpec((tq, width), lambda h, i: (i, h)),
        out_shape=jax.ShapeDtypeStruct((t, FOX_HEADS * FOX_DIM), BF16),
        compiler_params=_params(2),
        name="fox_attention",
    )(qkv, qkv, qkv, dcum, dcum_t)


def _swiglu_half(h, hn, norm_w, w_in, w_out, layer):
    n_main = (D_FF // FFN_TILE) * FFN_TILE
    n_tail = D_FF - n_main
    if hn is None:
        act_tail, hn = ffn_in(h, w_in, layer, n_main, n_tail, n_tail, norm_w=norm_w)
    else:
        act_tail = ffn_in(hn, w_in, layer, n_main, n_tail, n_tail)
    act_main = ffn_in(hn, w_in, layer, 0, n_main, FFN_TILE)
    return matmul_residual(h, w_out, layer, [(act_main, 0), (act_tail, n_main)], scale=0.5)


def _ssd_hgrn_layer(h, norm_w, w_in, layer, conv_w, conv_b, dt_bias, a_log, d_skip, ssd_norm_w, lb, hgrn_norm_w, w_out):
    d = D_MODEL
    nb = SSD_GROUPS * SSD_STATE
    o_dt = 2 * d + 2 * nb
    o_q = o_dt + SSD_HEADS
    tile = 1024 // SUBLANES
    wt = jnp.swapaxes(w_in, 1, 2)
    dt_raw, hn = matmul_nt(h, wt, layer, F32, LANES, lambda j: o_dt // SUBLANES, tn=LANES, norm_w=norm_w)
    proj = matmul_nt(hn, wt, layer, BF16, o_dt, lambda j: j * tile)
    first = o_q // SUBLANES
    qvg = matmul_nt(hn, wt, layer, BF16, 3 * d, lambda j: first + (j + jnp.where(j >= 2, 2, 0)) * tile)
    f_raw = matmul_nt(hn, wt, layer, F32, d, lambda j: first + (j + 2) * tile)
    pad_heads = lambda v: jnp.pad(v, (0, LANES - SSD_HEADS)).reshape(1, LANES)
    y_a = ssd_mixer(proj, dt_raw, conv_w[:, :d], conv_b[:d].reshape(1, d), conv_w[:, d:],
                    conv_b[d:].reshape(1, 2 * nb), pad_heads(dt_bias), pad_heads(a_log),
                    jnp.repeat(d_skip, SSD_HEAD_DIM).reshape(1, d), ssd_norm_w.reshape(1, d))
    y_b = hgrn_mixer(qvg, f_raw, lb.reshape(1, d), hgrn_norm_w.reshape(1, d))
    return matmul_residual(h, w_out, layer, [(y_a, 0), (y_b, d)], tm=1024)


def _fox_layer(h, norm_w, w_in, layer, b_f, w_out):
    d = D_MODEL
    t = h.shape[0]
    tile = 1024 // SUBLANES
    wt = jnp.swapaxes(w_in, 1, 2)
    f_raw, hn = matmul_nt(h, wt, layer, F32, LANES, lambda j: (3 * d + FOX_HEADS - LANES) // SUBLANES, tn=LANES,
                          norm_w=norm_w)
    qkv = matmul_nt(hn, wt, layer, BF16, 3 * d, lambda j: j * tile, scaled_blocks=d // 1024,
                    scale=LOG2E * FOX_DIM ** -0.5)
    b_pad = jnp.pad(b_f, (FOX_F_LANE0, 0)).reshape(1, LANES)
    dcum, dcum_t = logf_cumsum(f_raw, b_pad)
    o = fox_attention(qkv, dcum, dcum_t[FOX_F_LANE0:].reshape(FOX_HEADS, 1, t))
    return matmul_residual(h, w_out, layer, [(o, 0)], tm=1024)


def kernel(x, p, ffn1_norm, ffn1_w_in, ffn1_w_out, mix_norm, ab_w_in, ssd_conv_w, ssd_conv_b, ssd_dt_bias,
           ssd_a_log, ssd_d, ssd_norm, hgrn_lb_logits, hgrn_norm, ab_w_out, fox_w_in, fox_b_f, fox_w_out,
           ffn2_norm, ffn2_w_in, ffn2_w_out, ple_gate_norm, ple_w_gate, ple_w_up, ple_norm, final_norm):
    bsz, t, d = x.shape
    depth = p.shape[0]
    assert bsz == 1 and d == D_MODEL
    lb_all = jnp.cumsum(jax.nn.softmax(hgrn_lb_logits.astype(F32), axis=0), axis=0)
    h = x.reshape(t, d)
    hn = None
    for i in range(depth):
        j = i // 2
        h = _swiglu_half(h, hn, ffn1_norm[i], ffn1_w_in, ffn1_w_out, i)
        if i % 2 == 0:
            h = _ssd_hgrn_layer(h, mix_norm[i], ab_w_in, j, ssd_conv_w[j], ssd_conv_b[j], ssd_dt_bias[j],
                                ssd_a_log[j], ssd_d[j], ssd_norm[j], lb_all[i], hgrn_norm[j], ab_w_out)
        else:
            h = _fox_layer(h, mix_norm[i], fox_w_in, j, fox_b_f[j], fox_w_out)
        h = _swiglu_half(h, None, ffn2_norm[i], ffn2_w_in, ffn2_w_out, i)
        final = i == depth - 1
        next_w = final_norm if final else ffn1_norm[i + 1]
        outs = ple_add(h, p, ple_w_gate, ple_w_up, i, ple_gate_norm[i], ple_norm[i], next_w, final)
        if final:
            return outs[0].reshape(bsz, t, d)
        h, hn = outs
```

```python
import functools
import math

import jax
import jax.numpy as jnp
from jax import lax
from jax.experimental import pallas as pl
from jax.experimental.pallas import tpu as pltpu

F32 = jnp.float32
BF16 = jnp.bfloat16
EPS = 1e-6

D_MODEL = 2048
D_FF = 5504
SSD_HEADS = 32
SSD_HEAD_DIM = 64
SSD_GROUPS = 4
SSD_STATE = 128
SSD_GROUP_WIDTH = 512
SSD_CONV = 4
HGRN_HEADS = 16
HGRN_DIM = 128
FOX_HEADS = 16
FOX_DIM = 128
LANES = 128
SUBLANES = 8
CONV_HALO = 16

SSD_ROWS = 128
HGRN_ROWS = 64
HGRN_CHUNKS_PER_STEP = 4
FOX_Q_BLOCK = 1024
FOX_K_BLOCK = 2048
FOX_PACK = 2
FOX_F_LANE0 = LANES - FOX_HEADS
FFN_TILE = 512
VMEM_LIMIT = 56 * 1024 * 1024
LOG2E = math.log2(math.e)


def _params(n_axes, vmem=VMEM_LIMIT):
    return pltpu.CompilerParams(dimension_semantics=("arbitrary",) * n_axes, vmem_limit_bytes=vmem)


def _silu(x):
    return x * jax.nn.sigmoid(x)


def _softplus(x):
    return jnp.maximum(x, 0.0) + jnp.log(1.0 + jnp.exp(-jnp.abs(x)))


def _dot(a, b):
    return jnp.dot(a, b, preferred_element_type=F32)


def _dot_nt(a, b):
    return lax.dot_general(a, b, (((1,), (1,)), ((), ())), preferred_element_type=F32)


def _dot_tn(a, b):
    return lax.dot_general(a, b, (((0,), (0,)), ((), ())), preferred_element_type=F32)


def _dot_f32(a, b):
    return jnp.dot(a, b, preferred_element_type=F32, precision=lax.Precision.HIGHEST)


def _tril(n):
    r = lax.broadcasted_iota(jnp.int32, (n, n), 0)
    c = lax.broadcasted_iota(jnp.int32, (n, n), 1)
    return r >= c


def _weight_window(layer, rows, width, row_tile, col_tile, buffers=2):
    return pl.BlockSpec((pl.Element(1), pl.Element(rows), pl.Element(width)),
                        lambda j, i: (layer, row_tile(j) * SUBLANES, col_tile(j) * LANES),
                        pipeline_mode=pl.Buffered(buffers))


def _cast_weights_once(pairs):
    @pl.when(pl.program_id(1) == 0)
    def _():
        for w_ref, wb_ref in pairs:
            wb_ref[...] = w_ref[0].astype(BF16)


def _normed_rows(x_ref, nw_ref, hn_ref):
    x = x_ref[...]
    xn = (x * lax.rsqrt(jnp.mean(x * x, axis=-1, keepdims=True) + EPS) * nw_ref[...]).astype(BF16)
    hn_ref[...] = xn
    return xn


def _mm_nt_kernel(x_ref, w_ref, *refs, scaled_blocks, scale, normalize):
    wb_ref = refs[-1]
    _cast_weights_once([(w_ref, wb_ref)])
    if normalize:
        nw_ref, o_ref, hn_ref = refs[:3]
        x = _normed_rows(x_ref, nw_ref, hn_ref)
    else:
        o_ref = refs[0]
        x = x_ref[...]
    acc = _dot_nt(x, wb_ref[...])
    if scaled_blocks:
        acc = acc * jnp.where(pl.program_id(0) < scaled_blocks, scale, 1.0)
    o_ref[...] = acc.astype(o_ref.dtype)


def matmul_nt(x, wt, layer, out_dtype, n_out, row_tile, tn=1024, tm=1024, scaled_blocks=0, scale=1.0, norm_w=None):
    m, k = x.shape
    normalize = norm_w is not None
    assert not normalize or n_out == tn
    in_specs = [pl.BlockSpec((tm, k), lambda j, i: (i, 0)), _weight_window(layer, tn, k, row_tile, lambda j: 0)]
    out_specs = [pl.BlockSpec((tm, tn), lambda j, i: (i, j))]
    out_shape = [jax.ShapeDtypeStruct((m, n_out), out_dtype)]
    args = [x, wt]
    if normalize:
        in_specs.append(pl.BlockSpec((1, k), lambda j, i: (0, 0)))
        out_specs.append(pl.BlockSpec((tm, k), lambda j, i: (i, 0)))
        out_shape.append(jax.ShapeDtypeStruct((m, k), BF16))
        args.append(norm_w.reshape(1, k))
    outs = pl.pallas_call(
        functools.partial(_mm_nt_kernel, scaled_blocks=scaled_blocks, scale=scale, normalize=normalize),
        grid=(n_out // tn, m // tm),
        in_specs=in_specs,
        out_specs=out_specs,
        out_shape=out_shape,
        scratch_shapes=[pltpu.VMEM((tn, k), BF16)],
        compiler_params=_params(2),
        name="matmul_nt",
    )(*args)
    return outs if normalize else outs[0]


def _ffn_in_kernel(x_ref, wg_ref, wu_ref, *refs, normalize, cast_out_weight):
    refs = list(refs)
    nw_ref = refs.pop(0) if normalize else None
    wo_ref = refs.pop(0) if cast_out_weight else None
    o_ref = refs.pop(0)
    hn_ref = refs.pop(0) if normalize else None
    wob_ref = refs.pop(0) if cast_out_weight else None
    wgb_ref, wub_ref = refs
    _cast_weights_once([(wg_ref, wgb_ref), (wu_ref, wub_ref)])
    x = _normed_rows(x_ref, nw_ref, hn_ref) if normalize else x_ref[...]
    if cast_out_weight:
        wob_ref[...] = wo_ref[0].astype(BF16)
    gate = _dot(x, wgb_ref[...])
    up = _dot(x, wub_ref[...])
    o_ref[...] = (_silu(gate) * up).astype(o_ref.dtype)


def ffn_in(x, w_in, layer, col0, n_out, tn, tm=1024, norm_w=None, w_out=None):
    m, k = x.shape
    normalize = norm_w is not None
    cast_out_weight = w_out is not None
    assert not normalize or n_out == tn
    blocks_per_tile = tn // LANES
    gate_col = lambda j: col0 // LANES + j * blocks_per_tile
    up_col = lambda j: (D_FF + col0) // LANES + j * blocks_per_tile
    in_specs = [pl.BlockSpec((tm, k), lambda j, i: (i, 0)),
                _weight_window(layer, k, tn, lambda j: 0, gate_col),
                _weight_window(layer, k, tn, lambda j: 0, up_col)]
    out_specs = [pl.BlockSpec((tm, tn), lambda j, i: (i, j))]
    out_shape = [jax.ShapeDtypeStruct((m, n_out), BF16)]
    args = [x, w_in, w_in]
    if normalize:
        in_specs.append(pl.BlockSpec((1, k), lambda j, i: (0, 0)))
        out_specs.append(pl.BlockSpec((tm, k), lambda j, i: (i, 0)))
        out_shape.append(jax.ShapeDtypeStruct((m, k), BF16))
        args.append(norm_w.reshape(1, k))
    if cast_out_weight:
        cols_out = w_out.shape[2]
        steps_i = m // tm
        slab = n_out // ((n_out // tn) * steps_i)
        pack = 2 * SUBLANES
        assert slab % pack == 0 and col0 % pack == 0
        in_specs.append(pl.BlockSpec(
            (pl.Element(1), pl.Element(slab), pl.Element(cols_out)),
            lambda j, i: (layer, (col0 // pack + (j * steps_i + i) * (slab // pack)) * pack, 0)))
        out_specs.append(pl.BlockSpec((slab, cols_out), lambda j, i: (j * steps_i + i, 0)))
        out_shape.append(jax.ShapeDtypeStruct((n_out, cols_out), BF16))
        args.append(w_out)
    outs = pl.pallas_call(
        functools.partial(_ffn_in_kernel, normalize=normalize, cast_out_weight=cast_out_weight),
        grid=(n_out // tn, m // tm),
        in_specs=in_specs,
        out_specs=out_specs,
        out_shape=out_shape,
        scratch_shapes=[pltpu.VMEM((k, tn), BF16), pltpu.VMEM((k, tn), BF16)],
        compiler_params=_params(2),
        name="ffn_in",
    )(*args)
    return outs if normalize or cast_out_weight else outs[0]


def _mm_resid_kernel(*refs, n_pairs, scale, staged):
    h_ref = refs[0]
    a_refs = refs[1:1 + n_pairs]
    w_refs = refs[1 + n_pairs:1 + 2 * n_pairs]
    o_ref = refs[1 + 2 * n_pairs]
    if staged:
        wb_refs = refs[2 + 2 * n_pairs:]
        _cast_weights_once(list(zip(w_refs, wb_refs)))
        weights = [wb_ref[...] for wb_ref in wb_refs]
    else:
        weights = [w_ref[0] for w_ref in w_refs]
    acc = _dot(a_refs[0][...], weights[0])
    for a_ref, w in zip(a_refs[1:], weights[1:]):
        acc = acc + _dot(a_ref[...], w)
    if scale != 1.0:
        acc = scale * acc
    o_ref[...] = h_ref[...] + acc


def matmul_residual(h, w, layer, pieces, scale=1.0, tm=512, tn=512, weight_buffers=2):
    m, n = h.shape
    staged = w is not None
    if not staged:
        w_list = [wi.reshape(1, *wi.shape) for _, wi in pieces]
        pieces = [(a, 0) for a, _ in pieces]
    else:
        w_list = [w] * len(pieces)
    blocks_per_tile = tn // LANES
    a_specs = [pl.BlockSpec((tm, a.shape[1]), lambda j, i: (i, 0)) for a, _ in pieces]
    w_specs = [_weight_window(layer, a.shape[1], tn, lambda j, r=row0 // SUBLANES: r, lambda j: j * blocks_per_tile,
                              weight_buffers) for a, row0 in pieces]
    return pl.pallas_call(
        functools.partial(_mm_resid_kernel, n_pairs=len(pieces), scale=scale, staged=staged),
        grid=(n // tn, m // tm),
        in_specs=[pl.BlockSpec((tm, tn), lambda j, i: (i, j))] + a_specs + w_specs,
        out_specs=pl.BlockSpec((tm, tn), lambda j, i: (i, j)),
        out_shape=jax.ShapeDtypeStruct((m, n), F32),
        scratch_shapes=[pltpu.VMEM((a.shape[1], tn), BF16) for a, _ in pieces] if staged else [],
        compiler_params=_params(2),
        name="matmul_residual",
    )(h, *[a for a, _ in pieces], *w_list)


def _ple_kernel(h_ref, p_ref, wg_ref, wu_ref, gn_ref, pn_ref, nn_ref, *refs, final):
    wgb_ref, wub_ref = refs[-2:]
    out_refs = refs[:-2]

    @pl.when(pl.program_id(0) == 0)
    def _():
        wgb_ref[...] = wg_ref[...].astype(BF16)
        wub_ref[...] = wu_ref[...].astype(BF16)

    def normed(x, w_ref):
        return x * lax.rsqrt(jnp.mean(x * x, axis=-1, keepdims=True) + EPS) * w_ref[...]

    h = h_ref[...]
    gate = jax.nn.sigmoid(_dot(normed(h, gn_ref).astype(BF16), wgb_ref[...]))
    emb = normed(_dot(p_ref[...].astype(BF16), wub_ref[...]), pn_ref)
    h = h + emb * gate
    if not final:
        out_refs[0][...] = h
    out_refs[-1][...] = normed(h, nn_ref).astype(out_refs[-1].dtype)


def ple_add(h, p, w_gate, w_up, layer, gate_norm_w, post_norm_w, next_norm_w, final, tm=256):
    m, d = h.shape
    pd = p.shape[-1]
    row = lambda i: (i, 0)
    fixed = lambda i: (0, 0)
    resident = functools.partial(pl.BlockSpec, index_map=lambda i: (layer, 0, 0), pipeline_mode=pl.Buffered(1))
    out_dtypes = [F32] if final else [F32, BF16]
    return pl.pallas_call(
        functools.partial(_ple_kernel, final=final),
        grid=(m // tm,),
        in_specs=[pl.BlockSpec((tm, d), row),
                  pl.BlockSpec((None, None, tm, pd), lambda i: (layer, 0, i, 0)),
                  resident((None, d, d)), resident((None, pd, d)),
                  pl.BlockSpec((1, d), fixed), pl.BlockSpec((1, d), fixed), pl.BlockSpec((1, d), fixed)],
        out_specs=[pl.BlockSpec((tm, d), row) for _ in out_dtypes],
        out_shape=[jax.ShapeDtypeStruct((m, d), dt) for dt in out_dtypes],
        scratch_shapes=[pltpu.VMEM((d, d), BF16), pltpu.VMEM((pd, d), BF16)],
        compiler_params=_params(1),
        name="ple_add",
    )(h, p, w_gate, w_up, gate_norm_w.reshape(1, d), post_norm_w.reshape(1, d), next_norm_w.reshape(1, d))


def _ssd_kernel(z_ref, xs_ref, bc_ref, dt_ref, cwx_ref, cbx_ref, cwbc_ref, cbbc_ref, dtb_ref, alog_ref,
                dskip_ref, nw_ref, o_ref, xhalo_ref, bchalo_ref, st_ref, spread_ref):
    rows = SSD_ROWS

    @pl.when(pl.program_id(0) == 0)
    def _():
        xhalo_ref[...] = jnp.zeros(xhalo_ref.shape, BF16)
        bchalo_ref[...] = jnp.zeros(bchalo_ref.shape, BF16)
        st_ref[...] = jnp.zeros(st_ref.shape, F32)
        head = lax.broadcasted_iota(jnp.int32, spread_ref.shape, 0)
        lane = lax.broadcasted_iota(jnp.int32, spread_ref.shape, 1)
        spread_ref[...] = jnp.where(lane // SSD_HEAD_DIM == head, 1.0, 0.0).astype(BF16)

    out_row = lax.broadcasted_iota(jnp.int32, (rows, CONV_HALO + rows), 0)
    in_row = lax.broadcasted_iota(jnp.int32, (rows, CONV_HALO + rows), 1)
    shifts = [jnp.where(in_row == out_row + CONV_HALO - back, 1.0, 0.0).astype(BF16)
              for back in range(SSD_CONV - 1, 0, -1)]

    def conv_silu(halo_ref, raw_ref, w_ref, b_ref):
        raw = raw_ref[...]
        ext = jnp.concatenate([halo_ref[...], raw], axis=0)
        acc = b_ref[...] + raw.astype(F32) * w_ref[SSD_CONV - 1:SSD_CONV, :]
        for k, shift in enumerate(shifts):
            acc = acc + _dot(shift, ext) * w_ref[k:k + 1, :]
        halo_ref[...] = raw[rows - CONV_HALO:, :]
        return _silu(acc)

    xs = conv_silu(xhalo_ref, xs_ref, cwx_ref, cbx_ref)
    bc = conv_silu(bchalo_ref, bc_ref, cwbc_ref, cbbc_ref)

    is_head = lax.broadcasted_iota(jnp.int32, (rows, LANES), 1) < SSD_HEADS
    dt = jnp.where(is_head, _softplus(dt_ref[...] + dtb_ref[...]), 0.0)
    a = -jnp.exp(alog_ref[...])
    causal = _tril(rows)
    cum = _dot_f32(causal.astype(F32), dt * a)
    cum_t = cum.T
    low_half = lax.broadcasted_iota(jnp.int32, (rows, LANES), 1) < SSD_HEAD_DIM

    def per_lane(v, terms):
        out, rest = None, v
        for _ in range(terms):
            piece = rest.astype(BF16)
            rest = rest - piece.astype(F32)
            moved = _dot(piece, spread_ref[...])
            out = moved if out is None else out + moved
        return out

    dt_x = per_lane(dt, 2)
    cum_x = per_lane(cum, 3)

    pairs_per_group = SSD_GROUP_WIDTH // LANES
    for g in range(SSD_GROUPS):
        b_g = bc[:, g * SSD_STATE:(g + 1) * SSD_STATE]
        c_g = bc[:, (SSD_GROUPS + g) * SSD_STATE:(SSD_GROUPS + g + 1) * SSD_STATE].astype(BF16)
        cb = _dot_nt(c_g, b_g.astype(BF16))
        st = st_ref[g]
        y_off = _dot(c_g, st.astype(BF16))
        y_tiles, xw_tiles, dec_tiles = [], [], []
        for jj in range(pairs_per_group):
            j = g * pairs_per_group + jj
            h0 = 2 * j
            sl = slice(j * LANES, (j + 1) * LANES)
            x_p = xs[:, sl]
            cum_p = cum_x[:, sl]
            xdt = x_p * dt_x[:, sl]
            xdt_b = xdt.astype(BF16)
            halves = []
            for h in (h0, h0 + 1):
                diff = cum[:, h:h + 1] - cum_t[h:h + 1, :]
                decay = jnp.exp(jnp.where(causal, diff, -jnp.inf))
                halves.append(_dot((cb * decay).astype(BF16), xdt_b))
            y_diag = jnp.where(low_half, halves[0], halves[1])
            cum_last = cum_p[rows - 1:rows, :]
            xw_tiles.append((xdt * jnp.exp(cum_last - cum_p)).astype(BF16))
            dec_tiles.append(jnp.exp(cum_last))
            y = y_diag + y_off[:, jj * LANES:(jj + 1) * LANES] * jnp.exp(cum_p) + x_p * dskip_ref[:, sl]
            y_tiles.append(y * _silu(z_ref[:, sl].astype(F32)))
        xw = jnp.concatenate(xw_tiles, axis=1)
        st_ref[g] = st * jnp.concatenate(dec_tiles, axis=1) + _dot(b_g.T.astype(BF16), xw)
        ss = jnp.sum(y_tiles[0] * y_tiles[0], axis=-1, keepdims=True)
        for t in y_tiles[1:]:
            ss = ss + jnp.sum(t * t, axis=-1, keepdims=True)
        inv = lax.rsqrt(ss * (1.0 / SSD_GROUP_WIDTH) + EPS)
        for jj in range(pairs_per_group):
            sl = slice((g * pairs_per_group + jj) * LANES, (g * pairs_per_group + jj + 1) * LANES)
            o_ref[:, sl] = (y_tiles[jj] * inv * nw_ref[:, sl]).astype(o_ref.dtype)


def ssd_mixer(proj, dt_raw, conv_wx, conv_bx, conv_wbc, conv_bbc, dt_bias, a_log, d_skip_x, norm_w):
    t = proj.shape[0]
    rows = SSD_ROWS
    d = D_MODEL
    nbc = 2 * SSD_GROUPS * SSD_STATE
    fixed = lambda i: (0, 0)
    return pl.pallas_call(
        _ssd_kernel,
        grid=(t // rows,),
        in_specs=[pl.BlockSpec((rows, d), lambda i: (i, 0)),
                  pl.BlockSpec((rows, d), lambda i: (i, 1)),
                  pl.BlockSpec((rows, nbc), lambda i: (i, 2 * d // nbc)),
                  pl.BlockSpec((rows, LANES), lambda i: (i, 0)),
                  pl.BlockSpec((SSD_CONV, d), fixed), pl.BlockSpec((1, d), fixed),
                  pl.BlockSpec((SSD_CONV, nbc), fixed), pl.BlockSpec((1, nbc), fixed),
                  pl.BlockSpec((1, LANES), fixed), pl.BlockSpec((1, LANES), fixed),
                  pl.BlockSpec((1, d), fixed), pl.BlockSpec((1, d), fixed)],
        out_specs=pl.BlockSpec((rows, d), lambda i: (i, 0)),
        out_shape=jax.ShapeDtypeStruct((t, d), BF16),
        scratch_shapes=[pltpu.VMEM((CONV_HALO, d), BF16),
                        pltpu.VMEM((CONV_HALO, nbc), BF16),
                        pltpu.VMEM((SSD_GROUPS, SSD_STATE, SSD_GROUP_WIDTH), F32),
                        pltpu.VMEM((LANES, d), BF16)],
        compiler_params=_params(1),
        name="ssd_mixer",
    )(proj, proj, proj, dt_raw, conv_wx, conv_bx, conv_wbc, conv_bbc, dt_bias, a_log, d_skip_x, norm_w)


def _hgrn_kernel(q_ref, v_ref, g_ref, f_ref, lb_ref, nw_ref, o_ref, st_ref):
    rows = HGRN_ROWS

    @pl.when(pl.program_id(0) == 0)
    def _():
        st_ref[...] = jnp.zeros(st_ref.shape, F32)

    lb = lb_ref[...]
    causal = _tril(rows)
    tri = causal.astype(F32)
    for c in range(HGRN_CHUNKS_PER_STEP):
        r = slice(c * rows, (c + 1) * rows)
        f = lb + (1.0 - lb) * jax.nn.sigmoid(f_ref[r, :])
        k = 1.0 - f
        cum = _dot_f32(tri, jnp.log(f))
        qf = _silu(q_ref[r, :].astype(F32))
        mid = cum[rows // 2 - 1:rows // 2, :]
        last = cum[rows - 1:rows, :]
        q_rel = (qf * jnp.exp(cum - mid)).astype(BF16)
        k_rel = (k * jnp.exp(mid - cum)).astype(BF16)
        k_end = (k * jnp.exp(last - cum)).astype(BF16)
        q_dec = (qf * jnp.exp(cum)).astype(BF16)
        chunk_decay = jnp.exp(last)
        for h in range(HGRN_HEADS):
            sl = slice(h * HGRN_DIM, (h + 1) * HGRN_DIM)
            v_h = v_ref[r, sl]
            att = jnp.where(causal, _dot_nt(q_rel[:, sl], k_rel[:, sl]), 0.0)
            st = st_ref[h]
            o = _dot(att.astype(BF16), v_h) + _dot_nt(q_dec[:, sl], st.astype(BF16))
            st_ref[h] = st * chunk_decay[:, sl] + _dot_tn(v_h, k_end[:, sl])
            on = o * lax.rsqrt(jnp.mean(o * o, axis=-1, keepdims=True) + EPS) * nw_ref[:, sl]
            o_ref[r, sl] = (on * _silu(g_ref[r, sl].astype(F32))).astype(o_ref.dtype)


def hgrn_mixer(qvg, f_raw, lb, norm_w):
    t = qvg.shape[0]
    rows = HGRN_ROWS * HGRN_CHUNKS_PER_STEP
    d = D_MODEL
    fixed = lambda i: (0, 0)
    return pl.pallas_call(
        _hgrn_kernel,
        grid=(t // rows,),
        in_specs=[pl.BlockSpec((rows, d), lambda i: (i, 0)),
                  pl.BlockSpec((rows, d), lambda i: (i, 1)),
                  pl.BlockSpec((rows, d), lambda i: (i, 2)),
                  pl.BlockSpec((rows, d), lambda i: (i, 0)),
                  pl.BlockSpec((1, d), fixed), pl.BlockSpec((1, d), fixed)],
        out_specs=pl.BlockSpec((rows, d), lambda i: (i, 0)),
        out_shape=jax.ShapeDtypeStruct((t, d), BF16),
        scratch_shapes=[pltpu.VMEM((HGRN_HEADS, HGRN_DIM, HGRN_DIM), F32)],
        compiler_params=_params(1),
        name="hgrn_mixer",
    )(qvg, qvg, qvg, f_raw, lb, norm_w)


def _logf_cumsum_kernel(f_ref, b_ref, o_ref, ot_ref, carry_ref):
    rows = f_ref.shape[0]

    @pl.when(pl.program_id(0) == 0)
    def _():
        carry_ref[...] = jnp.zeros(carry_ref.shape, F32)

    log_f = -_softplus(-(f_ref[...] + b_ref[...]))
    c = _dot_f32(_tril(rows).astype(F32), log_f) + carry_ref[...]
    carry_ref[...] = c[rows - 1:rows, :]
    c2 = c * LOG2E
    o_ref[...] = c2
    ot_ref[...] = c2.T


def logf_cumsum(f_raw, b_f, rows=512):
    t = f_raw.shape[0]
    return pl.pallas_call(
        _logf_cumsum_kernel,
        grid=(t // rows,),
        in_specs=[pl.BlockSpec((rows, LANES), lambda i: (i, 0)), pl.BlockSpec((1, LANES), lambda i: (0, 0))],
        out_specs=[pl.BlockSpec((rows, LANES), lambda i: (i, 0)), pl.BlockSpec((LANES, rows), lambda i: (0, i))],
        out_shape=[jax.ShapeDtypeStruct((t, LANES), F32), jax.ShapeDtypeStruct((LANES, t), F32)],
        scratch_shapes=[pltpu.VMEM((1, LANES), F32)],
        compiler_params=_params(1),
        name="logf_cumsum",
    )(f_raw, b_f)


def _fox_kernel(q_ref, k_ref, v_ref, dkc_ref, dqr_ref, o_ref):
    tq, tk = FOX_Q_BLOCK, FOX_K_BLOCK
    half = tq // 2
    hp = pl.program_id(0)
    qi = pl.program_id(1)
    q_start = pl.multiple_of(qi * tq, tq)
    heads = []
    for hh in range(FOX_PACK):
        sl = slice(hh * FOX_DIM, (hh + 1) * FOX_DIM)
        heads.append((hh, sl, q_ref[:, sl], dqr_ref[hh, :, pl.ds(q_start, tq)]))

    def step(start, width, lo, carry, diagonal):
        out = []
        lane = lax.broadcasted_iota(jnp.int32, (width, LANES), 1)
        dk_all = dkc_ref[pl.ds(start, width), :]
        for (hh, sl, q, dq), (m, l, acc) in zip(heads, carry):
            k_b = k_ref[pl.ds(start, width), sl]
            v_b = v_ref[pl.ds(start, width), sl]
            dk = jnp.sum(jnp.where(lane == FOX_F_LANE0 + hp * FOX_PACK + hh, dk_all, 0.0), axis=-1, keepdims=True)
            s = _dot_nt(k_b, q[lo:]) - dk
            if diagonal:
                key = lax.broadcasted_iota(jnp.int32, s.shape, 0)
                qry = lax.broadcasted_iota(jnp.int32, s.shape, 1)
                s = jnp.where(qry >= key, s, -jnp.inf)
            m_new = jnp.maximum(m, jnp.max(s, axis=0, keepdims=True) + dq[:, lo:])
            alpha = jnp.exp2(m - m_new)
            p = jnp.exp2(s - (m_new - dq[:, lo:]))
            l = alpha * l + jnp.sum(p, axis=0, keepdims=True)
            acc = alpha * acc + _dot_tn(v_b, p.astype(BF16))
            out.append((m_new, l, acc))
        return tuple(out)

    init = tuple((jnp.full((1, tq), -jnp.inf, F32), jnp.zeros((1, tq), F32), jnp.zeros((FOX_DIM, tq), F32))
                 for _ in heads)
    n_wide = lax.shift_right_logical(qi, 1)
    carry = lax.fori_loop(0, n_wide, lambda ki, c: step(pl.multiple_of(ki * tk, tk), tk, 0, c, False), init)
    odd_start = pl.multiple_of(n_wide * tk, tk)
    carry = lax.fori_loop(0, qi & 1, lambda _, c: step(odd_start, tq, 0, c, False), carry)
    carry = step(q_start, half, 0, carry, True)
    hi = step(pl.multiple_of(q_start + half, half), half, half,
              tuple((m[:, half:], l[:, half:], acc[:, half:]) for m, l, acc in carry), True)
    for (hh, sl, _, _), (m, l, acc), (m_hi, l_hi, acc_hi) in zip(heads, carry, hi):
        l = jnp.concatenate([l[:, :half], l_hi], axis=1)
        acc = jnp.concatenate([acc[:, :half], acc_hi], axis=1)
        o_ref[:, sl] = (acc / l).T.astype(o_ref.dtype)


def fox_attention(qkv, dcum, dcum_t):
    t = qkv.shape[0]
    tq = FOX_Q_BLOCK
    width = FOX_PACK * FOX_DIM
    groups = FOX_HEADS // FOX_PACK
    return pl.pallas_call(
        _fox_kernel,
        grid=(groups, t // tq),
        in_specs=[pl.BlockSpec((tq, width), lambda h, i: (i, h)),
                  pl.BlockSpec((t, width), lambda h, i: (0, groups + h)),
                  pl.BlockSpec((t, width), lambda h, i: (0, 2 * groups + h)),
                  pl.BlockSpec((t, LANES), lambda h, i: (0, 0)),
                  pl.BlockSpec((FOX_PACK, 1, t), lambda h, i: (h, 0, 0))],
        out_specs=pl.BlockSpec((tq, width), lambda h, i: (i, h)),
        out_shape=jax.ShapeDtypeStruct((t, FOX_HEADS * FOX_DIM), BF16),
        compiler_params=_params(2),
        name="fox_attention",
    )(qkv, qkv, qkv, dcum, dcum_t)


def _swiglu_half(h, hn, norm_w, w_in, w_out, layer):
    n_main = (D_FF // FFN_TILE) * FFN_TILE
    n_tail = D_FF - n_main
    if hn is None:
        act_tail, hn, w_tail = ffn_in(h, w_in, layer, n_main, n_tail, n_tail, norm_w=norm_w, w_out=w_out)
    else:
        act_tail, w_tail = ffn_in(hn, w_in, layer, n_main, n_tail, n_tail, w_out=w_out)
    act_main, w_main = ffn_in(hn, w_in, layer, 0, n_main, FFN_TILE, w_out=w_out)
    return matmul_residual(h, None, 0, [(act_main, w_main), (act_tail, w_tail)], scale=0.5, tn=1024)


def _ssd_hgrn_layer(h, norm_w, w_in, layer, conv_w, conv_b, dt_bias, a_log, d_skip, ssd_norm_w, lb, hgrn_norm_w, w_out):
    d = D_MODEL
    nb = SSD_GROUPS * SSD_STATE
    o_dt = 2 * d + 2 * nb
    o_q = o_dt + SSD_HEADS
    tile = 1024 // SUBLANES
    wt = jnp.swapaxes(w_in, 1, 2)
    dt_raw, hn = matmul_nt(h, wt, layer, F32, LANES, lambda j: o_dt // SUBLANES, tn=LANES, norm_w=norm_w)
    proj = matmul_nt(hn, wt, layer, BF16, o_dt, lambda j: j * tile)
    first = o_q // SUBLANES
    qvg = matmul_nt(hn, wt, layer, BF16, 3 * d, lambda j: first + (j + jnp.where(j >= 2, 2, 0)) * tile)
    f_raw = matmul_nt(hn, wt, layer, F32, d, lambda j: first + (j + 2) * tile)
    pad_heads = lambda v: jnp.pad(v, (0, LANES - SSD_HEADS)).reshape(1, LANES)
    y_a = ssd_mixer(proj, dt_raw, conv_w[:, :d], conv_b[:d].reshape(1, d), conv_w[:, d:],
                    conv_b[d:].reshape(1, 2 * nb), pad_heads(dt_bias), pad_heads(a_log),
                    jnp.repeat(d_skip, SSD_HEAD_DIM).reshape(1, d), ssd_norm_w.reshape(1, d))
    y_b = hgrn_mixer(qvg, f_raw, lb.reshape(1, d), hgrn_norm_w.reshape(1, d))
    return matmul_residual(h, w_out, layer, [(y_a, 0), (y_b, d)], tn=1024, weight_buffers=1)


def _fox_layer(h, norm_w, w_in, layer, b_f, w_out):
    d = D_MODEL
    t = h.shape[0]
    tile = 1024 // SUBLANES
    wt = jnp.swapaxes(w_in, 1, 2)
    f_raw, hn = matmul_nt(h, wt, layer, F32, LANES, lambda j: (3 * d + FOX_HEADS - LANES) // SUBLANES, tn=LANES,
                          norm_w=norm_w)
    qkv = matmul_nt(hn, wt, layer, BF16, 3 * d, lambda j: j * tile, scaled_blocks=d // 1024,
                    scale=LOG2E * FOX_DIM ** -0.5)
    b_pad = jnp.pad(b_f, (FOX_F_LANE0, 0)).reshape(1, LANES)
    dcum, dcum_t = logf_cumsum(f_raw, b_pad)
    o = fox_attention(qkv, dcum, dcum_t[FOX_F_LANE0:].reshape(FOX_HEADS, 1, t))
    return matmul_residual(h, w_out, layer, [(o, 0)], tm=1024, tn=1024)


def kernel(x, p, ffn1_norm, ffn1_w_in, ffn1_w_out, mix_norm, ab_w_in, ssd_conv_w, ssd_conv_b, ssd_dt_bias,
           ssd_a_log, ssd_d, ssd_norm, hgrn_lb_logits, hgrn_norm, ab_w_out, fox_w_in, fox_b_f, fox_w_out,
           ffn2_norm, ffn2_w_in, ffn2_w_out, ple_gate_norm, ple_w_gate, ple_w_up, ple_norm, final_norm):
    bsz, t, d = x.shape
    depth = p.shape[0]
    assert bsz == 1 and d == D_MODEL
    lb_all = jnp.cumsum(jax.nn.softmax(hgrn_lb_logits.astype(F32), axis=0), axis=0)
    h = x.reshape(t, d)
    hn = None
    for i in range(depth):
        j = i // 2
        h = _swiglu_half(h, hn, ffn1_norm[i], ffn1_w_in, ffn1_w_out, i)
        if i % 2 == 0:
            h = _ssd_hgrn_layer(h, mix_norm[i], ab_w_in, j, ssd_conv_w[j], ssd_conv_b[j], ssd_dt_bias[j],
                                ssd_a_log[j], ssd_d[j], ssd_norm[j], lb_all[i], hgrn_norm[j], ab_w_out)
        else:
            h = _fox_layer(h, mix_norm[i], fox_w_in, j, fox_b_f[j], fox_w_out)
        h = _swiglu_half(h, None, ffn2_norm[i], ffn2_w_in, ffn2_w_out, i)
        final = i == depth - 1
        next_w = final_norm if final else ffn1_norm[i + 1]
        outs = ple_add(h, p, ple_w_gate, ple_w_up, i, ple_gate_norm[i], ple_norm[i], next_w, final)
        if final:
            return outs[0].reshape(bsz, t, d)
        h, hn = outs
```

```python
import functools
import math

import jax
import jax.numpy as jnp
from jax import lax
from jax.experimental import pallas as pl
from jax.experimental.pallas import tpu as pltpu

F32 = jnp.float32
BF16 = jnp.bfloat16
EPS = 1e-6

D_MODEL = 2048
D_FF = 5504
SSD_HEADS = 32
SSD_HEAD_DIM = 64
SSD_GROUPS = 4
SSD_STATE = 128
SSD_GROUP_WIDTH = 512
SSD_CONV = 4
HGRN_HEADS = 16
HGRN_DIM = 128
FOX_HEADS = 16
FOX_DIM = 128
LANES = 128
SUBLANES = 8
CONV_HALO = 16

SSD_ROWS = 128
SSD_BLOCKS_PER_STEP = 2
HGRN_ROWS = 64
HGRN_CHUNKS_PER_STEP = 8
FOX_Q_BLOCK = 1024
FOX_K_BLOCK = 2048
FOX_PACK = 2
FOX_F_LANE0 = LANES - FOX_HEADS
FFN_TILE = 512
VMEM_LIMIT = 56 * 1024 * 1024
LOG2E = math.log2(math.e)


def _params(n_axes, vmem=VMEM_LIMIT):
    return pltpu.CompilerParams(dimension_semantics=("arbitrary",) * n_axes, vmem_limit_bytes=vmem)


def _silu(x):
    return x * jax.nn.sigmoid(x)


def _softplus(x):
    return jnp.maximum(x, 0.0) + jnp.log(1.0 + jnp.exp(-jnp.abs(x)))


def _dot(a, b):
    return jnp.dot(a, b, preferred_element_type=F32)


def _dot_nt(a, b):
    return lax.dot_general(a, b, (((1,), (1,)), ((), ())), preferred_element_type=F32)


def _dot_tn(a, b):
    return lax.dot_general(a, b, (((0,), (0,)), ((), ())), preferred_element_type=F32)


def _dot_f32(a, b):
    return jnp.dot(a, b, preferred_element_type=F32, precision=lax.Precision.HIGHEST)


def _tril(n):
    r = lax.broadcasted_iota(jnp.int32, (n, n), 0)
    c = lax.broadcasted_iota(jnp.int32, (n, n), 1)
    return r >= c


def _weight_window(layer, rows, width, row_tile, col_tile, buffers=2):
    return pl.BlockSpec((pl.Element(1), pl.Element(rows), pl.Element(width)),
                        lambda j, i: (layer, row_tile(j) * SUBLANES, col_tile(j) * LANES),
                        pipeline_mode=pl.Buffered(buffers))


def _cast_weights_once(pairs):
    @pl.when(pl.program_id(1) == 0)
    def _():
        for w_ref, wb_ref in pairs:
            wb_ref[...] = w_ref[0].astype(BF16)


def _normed_rows(x_ref, nw_ref, hn_ref):
    x = x_ref[...]
    xn = (x * lax.rsqrt(jnp.mean(x * x, axis=-1, keepdims=True) + EPS) * nw_ref[...]).astype(BF16)
    hn_ref[...] = xn
    return xn


def _mm_nt_kernel(x_ref, w_ref, *refs, scaled_blocks, scale, normalize):
    wb_ref = refs[-1]
    _cast_weights_once([(w_ref, wb_ref)])
    if normalize:
        nw_ref, o_ref, hn_ref = refs[:3]
        x = _normed_rows(x_ref, nw_ref, hn_ref)
    else:
        o_ref = refs[0]
        x = x_ref[...]
    acc = _dot_nt(x, wb_ref[...])
    if scaled_blocks:
        acc = acc * jnp.where(pl.program_id(0) < scaled_blocks, scale, 1.0)
    o_ref[...] = acc.astype(o_ref.dtype)


def matmul_nt(x, wt, layer, out_dtype, n_out, row_tile, tn=1024, tm=1024, scaled_blocks=0, scale=1.0, norm_w=None):
    m, k = x.shape
    normalize = norm_w is not None
    assert not normalize or n_out == tn
    in_specs = [pl.BlockSpec((tm, k), lambda j, i: (i, 0)), _weight_window(layer, tn, k, row_tile, lambda j: 0)]
    out_specs = [pl.BlockSpec((tm, tn), lambda j, i: (i, j))]
    out_shape = [jax.ShapeDtypeStruct((m, n_out), out_dtype)]
    args = [x, wt]
    if normalize:
        in_specs.append(pl.BlockSpec((1, k), lambda j, i: (0, 0)))
        out_specs.append(pl.BlockSpec((tm, k), lambda j, i: (i, 0)))
        out_shape.append(jax.ShapeDtypeStruct((m, k), BF16))
        args.append(norm_w.reshape(1, k))
    outs = pl.pallas_call(
        functools.partial(_mm_nt_kernel, scaled_blocks=scaled_blocks, scale=scale, normalize=normalize),
        grid=(n_out // tn, m // tm),
        in_specs=in_specs,
        out_specs=out_specs,
        out_shape=out_shape,
        scratch_shapes=[pltpu.VMEM((tn, k), BF16)],
        compiler_params=_params(2),
        name="matmul_nt",
    )(*args)
    return outs if normalize else outs[0]


def _ffn_in_kernel(x_ref, wg_ref, wu_ref, *refs, normalize, cast_out_weight):
    refs = list(refs)
    nw_ref = refs.pop(0) if normalize else None
    wo_ref = refs.pop(0) if cast_out_weight else None
    o_ref = refs.pop(0)
    hn_ref = refs.pop(0) if normalize else None
    wob_ref = refs.pop(0) if cast_out_weight else None
    wgb_ref, wub_ref = refs
    _cast_weights_once([(wg_ref, wgb_ref), (wu_ref, wub_ref)])
    x = _normed_rows(x_ref, nw_ref, hn_ref) if normalize else x_ref[...]
    if cast_out_weight:
        wob_ref[...] = wo_ref[0].astype(BF16)
    gate = _dot(x, wgb_ref[...])
    up = _dot(x, wub_ref[...])
    o_ref[...] = (_silu(gate) * up).astype(o_ref.dtype)


def ffn_in(x, w_in, layer, col0, n_out, tn, tm=1024, norm_w=None, w_out=None):
    m, k = x.shape
    normalize = norm_w is not None
    cast_out_weight = w_out is not None
    assert not normalize or n_out == tn
    blocks_per_tile = tn // LANES
    gate_col = lambda j: col0 // LANES + j * blocks_per_tile
    up_col = lambda j: (D_FF + col0) // LANES + j * blocks_per_tile
    in_specs = [pl.BlockSpec((tm, k), lambda j, i: (i, 0)),
                _weight_window(layer, k, tn, lambda j: 0, gate_col),
                _weight_window(layer, k, tn, lambda j: 0, up_col)]
    out_specs = [pl.BlockSpec((tm, tn), lambda j, i: (i, j))]
    out_shape = [jax.ShapeDtypeStruct((m, n_out), BF16)]
    args = [x, w_in, w_in]
    if normalize:
        in_specs.append(pl.BlockSpec((1, k), lambda j, i: (0, 0)))
        out_specs.append(pl.BlockSpec((tm, k), lambda j, i: (i, 0)))
        out_shape.append(jax.ShapeDtypeStruct((m, k), BF16))
        args.append(norm_w.reshape(1, k))
    if cast_out_weight:
        cols_out = w_out.shape[2]
        steps_i = m // tm
        slab = n_out // ((n_out // tn) * steps_i)
        pack = 2 * SUBLANES
        assert slab % pack == 0 and col0 % pack == 0
        in_specs.append(pl.BlockSpec(
            (pl.Element(1), pl.Element(slab), pl.Element(cols_out)),
            lambda j, i: (layer, (col0 // pack + (j * steps_i + i) * (slab // pack)) * pack, 0)))
        out_specs.append(pl.BlockSpec((slab, cols_out), lambda j, i: (j * steps_i + i, 0)))
        out_shape.append(jax.ShapeDtypeStruct((n_out, cols_out), BF16))
        args.append(w_out)
    outs = pl.pallas_call(
        functools.partial(_ffn_in_kernel, normalize=normalize, cast_out_weight=cast_out_weight),
        grid=(n_out // tn, m // tm),
        in_specs=in_specs,
        out_specs=out_specs,
        out_shape=out_shape,
        scratch_shapes=[pltpu.VMEM((k, tn), BF16), pltpu.VMEM((k, tn), BF16)],
        compiler_params=_params(2),
        name="ffn_in",
    )(*args)
    return outs if normalize or cast_out_weight else outs[0]


def _mm_resid_kernel(*refs, n_pairs, scale, staged):
    h_ref = refs[0]
    a_refs = refs[1:1 + n_pairs]
    w_refs = refs[1 + n_pairs:1 + 2 * n_pairs]
    o_ref = refs[1 + 2 * n_pairs]
    if staged:
        wb_refs = refs[2 + 2 * n_pairs:]
        _cast_weights_once(list(zip(w_refs, wb_refs)))
        weights = [wb_ref[...] for wb_ref in wb_refs]
    else:
        weights = [w_ref[0] for w_ref in w_refs]
    acc = _dot(a_refs[0][...], weights[0])
    for a_ref, w in zip(a_refs[1:], weights[1:]):
        acc = acc + _dot(a_ref[...], w)
    if scale != 1.0:
        acc = scale * acc
    o_ref[...] = h_ref[...] + acc


def matmul_residual(h, w, layer, pieces, scale=1.0, tm=512, tn=512, weight_buffers=2):
    m, n = h.shape
    staged = w is not None
    if not staged:
        w_list = [wi.reshape(1, *wi.shape) for _, wi in pieces]
        pieces = [(a, 0) for a, _ in pieces]
    else:
        w_list = [w] * len(pieces)
    blocks_per_tile = tn // LANES
    a_specs = [pl.BlockSpec((tm, a.shape[1]), lambda j, i: (i, 0)) for a, _ in pieces]
    w_specs = [_weight_window(layer, a.shape[1], tn, lambda j, r=row0 // SUBLANES: r, lambda j: j * blocks_per_tile,
                              weight_buffers) for a, row0 in pieces]
    return pl.pallas_call(
        functools.partial(_mm_resid_kernel, n_pairs=len(pieces), scale=scale, staged=staged),
        grid=(n // tn, m // tm),
        in_specs=[pl.BlockSpec((tm, tn), lambda j, i: (i, j))] + a_specs + w_specs,
        out_specs=pl.BlockSpec((tm, tn), lambda j, i: (i, j)),
        out_shape=jax.ShapeDtypeStruct((m, n), F32),
        scratch_shapes=[pltpu.VMEM((a.shape[1], tn), BF16) for a, _ in pieces] if staged else [],
        compiler_params=_params(2),
        name="matmul_residual",
    )(h, *[a for a, _ in pieces], *w_list)


def _ple_kernel(h_ref, p_ref, wg_ref, wu_ref, gn_ref, pn_ref, nn_ref, *refs, final):
    wgb_ref, wub_ref = refs[-2:]
    out_refs = refs[:-2]

    @pl.when(pl.program_id(0) == 0)
    def _():
        wgb_ref[...] = wg_ref[...].astype(BF16)
        wub_ref[...] = wu_ref[...].astype(BF16)

    def normed(x, w_ref):
        return x * lax.rsqrt(jnp.mean(x * x, axis=-1, keepdims=True) + EPS) * w_ref[...]

    h = h_ref[...]
    gate = jax.nn.sigmoid(_dot(normed(h, gn_ref).astype(BF16), wgb_ref[...]))
    emb = normed(_dot(p_ref[...].astype(BF16), wub_ref[...]), pn_ref)
    h = h + emb * gate
    if not final:
        out_refs[0][...] = h
    out_refs[-1][...] = normed(h, nn_ref).astype(out_refs[-1].dtype)


def ple_add(h, p, w_gate, w_up, layer, gate_norm_w, post_norm_w, next_norm_w, final, tm=256):
    m, d = h.shape
    pd = p.shape[-1]
    row = lambda i: (i, 0)
    fixed = lambda i: (0, 0)
    resident = functools.partial(pl.BlockSpec, index_map=lambda i: (layer, 0, 0), pipeline_mode=pl.Buffered(1))
    out_dtypes = [F32] if final else [F32, BF16]
    return pl.pallas_call(
        functools.partial(_ple_kernel, final=final),
        grid=(m // tm,),
        in_specs=[pl.BlockSpec((tm, d), row),
                  pl.BlockSpec((None, None, tm, pd), lambda i: (layer, 0, i, 0)),
                  resident((None, d, d)), resident((None, pd, d)),
                  pl.BlockSpec((1, d), fixed), pl.BlockSpec((1, d), fixed), pl.BlockSpec((1, d), fixed)],
        out_specs=[pl.BlockSpec((tm, d), row) for _ in out_dtypes],
        out_shape=[jax.ShapeDtypeStruct((m, d), dt) for dt in out_dtypes],
        scratch_shapes=[pltpu.VMEM((d, d), BF16), pltpu.VMEM((pd, d), BF16)],
        compiler_params=_params(1),
        name="ple_add",
    )(h, p, w_gate, w_up, gate_norm_w.reshape(1, d), post_norm_w.reshape(1, d), next_norm_w.reshape(1, d))


def _ssd_kernel(z_ref, xs_ref, bc_ref, dt_ref, cwx_ref, cbx_ref, cwbc_ref, cbbc_ref, dtb_ref, alog_ref,
                dskip_ref, nw_ref, o_ref, xhalo_ref, bchalo_ref, st_ref, spread_ref):
    rows = SSD_ROWS

    @pl.when(pl.program_id(0) == 0)
    def _():
        xhalo_ref[...] = jnp.zeros(xhalo_ref.shape, BF16)
        bchalo_ref[...] = jnp.zeros(bchalo_ref.shape, BF16)
        st_ref[...] = jnp.zeros(st_ref.shape, F32)
        head = lax.broadcasted_iota(jnp.int32, spread_ref.shape, 0)
        lane = lax.broadcasted_iota(jnp.int32, spread_ref.shape, 1)
        spread_ref[...] = jnp.where(lane // SSD_HEAD_DIM == head, 1.0, 0.0).astype(BF16)

    out_row = lax.broadcasted_iota(jnp.int32, (rows, CONV_HALO + rows), 0)
    in_row = lax.broadcasted_iota(jnp.int32, (rows, CONV_HALO + rows), 1)
    shifts = [jnp.where(in_row == out_row + CONV_HALO - back, 1.0, 0.0).astype(BF16)
              for back in range(SSD_CONV - 1, 0, -1)]

    for blk in range(SSD_BLOCKS_PER_STEP):
        rr = slice(blk * rows, (blk + 1) * rows)

        def conv_silu(halo_ref, raw_ref, w_ref, b_ref):
            raw = raw_ref[rr, :]
            ext = jnp.concatenate([halo_ref[...], raw], axis=0)
            acc = b_ref[...] + raw.astype(F32) * w_ref[SSD_CONV - 1:SSD_CONV, :]
            for k, shift in enumerate(shifts):
                acc = acc + _dot(shift, ext) * w_ref[k:k + 1, :]
            halo_ref[...] = raw[rows - CONV_HALO:, :]
            return _silu(acc)

        xs = conv_silu(xhalo_ref, xs_ref, cwx_ref, cbx_ref)
        bc = conv_silu(bchalo_ref, bc_ref, cwbc_ref, cbbc_ref)

        is_head = lax.broadcasted_iota(jnp.int32, (rows, LANES), 1) < SSD_HEADS
        dt = jnp.where(is_head, _softplus(dt_ref[rr, :] + dtb_ref[...]), 0.0)
        a = -jnp.exp(alog_ref[...])
        causal = _tril(rows)
        cum = _dot_f32(causal.astype(F32), dt * a)
        cum_t = cum.T
        low_half = lax.broadcasted_iota(jnp.int32, (rows, LANES), 1) < SSD_HEAD_DIM

        def per_lane(v, terms):
            out, rest = None, v
            for _ in range(terms):
                piece = rest.astype(BF16)
                rest = rest - piece.astype(F32)
                moved = _dot(piece, spread_ref[...])
                out = moved if out is None else out + moved
            return out

        dt_x = per_lane(dt, 2)
        cum_x = per_lane(cum, 3)

        pairs_per_group = SSD_GROUP_WIDTH // LANES
        for g in range(SSD_GROUPS):
            b_g = bc[:, g * SSD_STATE:(g + 1) * SSD_STATE]
            c_g = bc[:, (SSD_GROUPS + g) * SSD_STATE:(SSD_GROUPS + g + 1) * SSD_STATE].astype(BF16)
            cb = _dot_nt(c_g, b_g.astype(BF16))
            st = st_ref[g]
            y_off = _dot(c_g, st.astype(BF16))
            y_tiles, xw_tiles, dec_tiles = [], [], []
            for jj in range(pairs_per_group):
                j = g * pairs_per_group + jj
                h0 = 2 * j
                sl = slice(j * LANES, (j + 1) * LANES)
                x_p = xs[:, sl]
                cum_p = cum_x[:, sl]
                xdt = x_p * dt_x[:, sl]
                xdt_b = xdt.astype(BF16)
                halves = []
                for h in (h0, h0 + 1):
                    diff = cum[:, h:h + 1] - cum_t[h:h + 1, :]
                    decay = jnp.exp(jnp.where(causal, diff, -jnp.inf))
                    halves.append(_dot((cb * decay).astype(BF16), xdt_b))
                y_diag = jnp.where(low_half, halves[0], halves[1])
                cum_last = cum_p[rows - 1:rows, :]
                xw_tiles.append((xdt * jnp.exp(cum_last - cum_p)).astype(BF16))
                dec_tiles.append(jnp.exp(cum_last))
                y = y_diag + y_off[:, jj * LANES:(jj + 1) * LANES] * jnp.exp(cum_p) + x_p * dskip_ref[:, sl]
                y_tiles.append(y * _silu(z_ref[rr, sl].astype(F32)))
            xw = jnp.concatenate(xw_tiles, axis=1)
            st_ref[g] = st * jnp.concatenate(dec_tiles, axis=1) + _dot(b_g.T.astype(BF16), xw)
            ss = jnp.sum(y_tiles[0] * y_tiles[0], axis=-1, keepdims=True)
            for t in y_tiles[1:]:
                ss = ss + jnp.sum(t * t, axis=-1, keepdims=True)
            inv = lax.rsqrt(ss * (1.0 / SSD_GROUP_WIDTH) + EPS)
            for jj in range(pairs_per_group):
                sl = slice((g * pairs_per_group + jj) * LANES, (g * pairs_per_group + jj + 1) * LANES)
                o_ref[rr, sl] = (y_tiles[jj] * inv * nw_ref[:, sl]).astype(o_ref.dtype)


def ssd_mixer(proj, dt_raw, conv_wx, conv_bx, conv_wbc, conv_bbc, dt_bias, a_log, d_skip_x, norm_w):
    t = proj.shape[0]
    rows = SSD_ROWS * SSD_BLOCKS_PER_STEP
    d = D_MODEL
    nbc = 2 * SSD_GROUPS * SSD_STATE
    fixed = lambda i: (0, 0)
    return pl.pallas_call(
        _ssd_kernel,
        grid=(t // rows,),
        in_specs=[pl.BlockSpec((rows, d), lambda i: (i, 0)),
                  pl.BlockSpec((rows, d), lambda i: (i, 1)),
                  pl.BlockSpec((rows, nbc), lambda i: (i, 2 * d // nbc)),
                  pl.BlockSpec((rows, LANES), lambda i: (i, 0)),
                  pl.BlockSpec((SSD_CONV, d), fixed), pl.BlockSpec((1, d), fixed),
                  pl.BlockSpec((SSD_CONV, nbc), fixed), pl.BlockSpec((1, nbc), fixed),
                  pl.BlockSpec((1, LANES), fixed), pl.BlockSpec((1, LANES), fixed),
                  pl.BlockSpec((1, d), fixed), pl.BlockSpec((1, d), fixed)],
        out_specs=pl.BlockSpec((rows, d), lambda i: (i, 0)),
        out_shape=jax.ShapeDtypeStruct((t, d), BF16),
        scratch_shapes=[pltpu.VMEM((CONV_HALO, d), BF16),
                        pltpu.VMEM((CONV_HALO, nbc), BF16),
                        pltpu.VMEM((SSD_GROUPS, SSD_STATE, SSD_GROUP_WIDTH), F32),
                        pltpu.VMEM((LANES, d), BF16)],
        compiler_params=_params(1),
        name="ssd_mixer",
    )(proj, proj, proj, dt_raw, conv_wx, conv_bx, conv_wbc, conv_bbc, dt_bias, a_log, d_skip_x, norm_w)


def _hgrn_kernel(q_ref, v_ref, g_ref, f_ref, lb_ref, nw_ref, o_ref, st_ref):
    rows = HGRN_ROWS

    @pl.when(pl.program_id(0) == 0)
    def _():
        st_ref[...] = jnp.zeros(st_ref.shape, F32)

    lb = lb_ref[...]
    causal = _tril(rows)
    tri = causal.astype(F32)
    for c in range(HGRN_CHUNKS_PER_STEP):
        r = slice(c * rows, (c + 1) * rows)
        f = lb + (1.0 - lb) * jax.nn.sigmoid(f_ref[r, :])
        k = 1.0 - f
        cum = _dot_f32(tri, jnp.log(f))
        qf = _silu(q_ref[r, :].astype(F32))
        mid = cum[rows // 2 - 1:rows // 2, :]
        last = cum[rows - 1:rows, :]
        q_rel = (qf * jnp.exp(cum - mid)).astype(BF16)
        k_rel = (k * jnp.exp(mid - cum)).astype(BF16)
        k_end = (k * jnp.exp(last - cum)).astype(BF16)
        q_dec = (qf * jnp.exp(cum)).astype(BF16)
        chunk_decay = jnp.exp(last)
        for h in range(HGRN_HEADS):
            sl = slice(h * HGRN_DIM, (h + 1) * HGRN_DIM)
            v_h = v_ref[r, sl]
            att = jnp.where(causal, _dot_nt(q_rel[:, sl], k_rel[:, sl]), 0.0)
            st = st_ref[h]
            o = _dot(att.astype(BF16), v_h) + _dot_nt(q_dec[:, sl], st.astype(BF16))
            st_ref[h] = st * chunk_decay[:, sl] + _dot_tn(v_h, k_end[:, sl])
            on = o * lax.rsqrt(jnp.mean(o * o, axis=-1, keepdims=True) + EPS) * nw_ref[:, sl]
            o_ref[r, sl] = (on * _silu(g_ref[r, sl].astype(F32))).astype(o_ref.dtype)


def hgrn_mixer(qvg, f_raw, lb, norm_w):
    t = qvg.shape[0]
    rows = HGRN_ROWS * HGRN_CHUNKS_PER_STEP
    d = D_MODEL
    fixed = lambda i: (0, 0)
    return pl.pallas_call(
        _hgrn_kernel,
        grid=(t // rows,),
        in_specs=[pl.BlockSpec((rows, d), lambda i: (i, 0)),
                  pl.BlockSpec((rows, d), lambda i: (i, 1)),
                  pl.BlockSpec((rows, d), lambda i: (i, 2)),
                  pl.BlockSpec((rows, d), lambda i: (i, 0)),
                  pl.BlockSpec((1, d), fixed), pl.BlockSpec((1, d), fixed)],
        out_specs=pl.BlockSpec((rows, d), lambda i: (i, 0)),
        out_shape=jax.ShapeDtypeStruct((t, d), BF16),
        scratch_shapes=[pltpu.VMEM((HGRN_HEADS, HGRN_DIM, HGRN_DIM), F32)],
        compiler_params=_params(1),
        name="hgrn_mixer",
    )(qvg, qvg, qvg, f_raw, lb, norm_w)


def _logf_cumsum_kernel(f_ref, b_ref, o_ref, ot_ref, carry_ref):
    rows = f_ref.shape[0]

    @pl.when(pl.program_id(0) == 0)
    def _():
        carry_ref[...] = jnp.zeros(carry_ref.shape, F32)

    log_f = -_softplus(-(f_ref[...] + b_ref[...]))
    c = _dot_f32(_tril(rows).astype(F32), log_f) + carry_ref[...]
    carry_ref[...] = c[rows - 1:rows, :]
    c2 = c * LOG2E
    o_ref[...] = c2
    ot_ref[...] = c2.T


def logf_cumsum(f_raw, b_f, rows=512):
    t = f_raw.shape[0]
    return pl.pallas_call(
        _logf_cumsum_kernel,
        grid=(t // rows,),
        in_specs=[pl.BlockSpec((rows, LANES), lambda i: (i, 0)), pl.BlockSpec((1, LANES), lambda i: (0, 0))],
        out_specs=[pl.BlockSpec((rows, LANES), lambda i: (i, 0)), pl.BlockSpec((LANES, rows), lambda i: (0, i))],
        out_shape=[jax.ShapeDtypeStruct((t, LANES), F32), jax.ShapeDtypeStruct((LANES, t), F32)],
        scratch_shapes=[pltpu.VMEM((1, LANES), F32)],
        compiler_params=_params(1),
        name="logf_cumsum",
    )(f_raw, b_f)


def _fox_kernel(q_ref, k_ref, v_ref, dkc_ref, dqr_ref, o_ref):
    tq, tk = FOX_Q_BLOCK, FOX_K_BLOCK
    half = tq // 2
    hp = pl.program_id(0)
    qi = pl.program_id(1)
    q_start = pl.multiple_of(qi * tq, tq)
    heads = []
    for hh in range(FOX_PACK):
        sl = slice(hh * FOX_DIM, (hh + 1) * FOX_DIM)
        heads.append((hh, sl, q_ref[:, sl], dqr_ref[hh, :, pl.ds(q_start, tq)]))

    def step(start, width, lo, carry, diagonal):
        out = []
        lane = lax.broadcasted_iota(jnp.int32, (width, LANES), 1)
        dk_all = dkc_ref[pl.ds(start, width), :]
        for (hh, sl, q, dq), (m, l, acc) in zip(heads, carry):
            k_b = k_ref[pl.ds(start, width), sl]
            v_b = v_ref[pl.ds(start, width), sl]
            dk = jnp.sum(jnp.where(lane == FOX_F_LANE0 + hp * FOX_PACK + hh, dk_all, 0.0), axis=-1, keepdims=True)
            s = _dot_nt(k_b, q[lo:]) - dk
            if diagonal:
                key = lax.broadcasted_iota(jnp.int32, s.shape, 0)
                qry = lax.broadcasted_iota(jnp.int32, s.shape, 1)
                s = jnp.where(qry >= key, s, -jnp.inf)
            m_new = jnp.maximum(m, jnp.max(s, axis=0, keepdims=True) + dq[:, lo:])
            alpha = jnp.exp2(m - m_new)
            p = jnp.exp2(s - (m_new - dq[:, lo:]))
            l = alpha * l + jnp.sum(p, axis=0, keepdims=True)
            acc = alpha * acc + _dot_tn(v_b, p.astype(BF16))
            out.append((m_new, l, acc))
        return tuple(out)

    init = tuple((jnp.full((1, tq), -jnp.inf, F32), jnp.zeros((1, tq), F32), jnp.zeros((FOX_DIM, tq), F32))
                 for _ in heads)
    n_wide = lax.shift_right_logical(qi, 1)
    carry = lax.fori_loop(0, n_wide, lambda ki, c: step(pl.multiple_of(ki * tk, tk), tk, 0, c, False), init)
    odd_start = pl.multiple_of(n_wide * tk, tk)
    carry = lax.fori_loop(0, qi & 1, lambda _, c: step(odd_start, tq, 0, c, False), carry)
    carry = step(q_start, half, 0, carry, True)
    hi = step(pl.multiple_of(q_start + half, half), half, half,
              tuple((m[:, half:], l[:, half:], acc[:, half:]) for m, l, acc in carry), True)
    for (hh, sl, _, _), (m, l, acc), (m_hi, l_hi, acc_hi) in zip(heads, carry, hi):
        l = jnp.concatenate([l[:, :half], l_hi], axis=1)
        acc = jnp.concatenate([acc[:, :half], acc_hi], axis=1)
        o_ref[:, sl] = (acc / l).T.astype(o_ref.dtype)


def fox_attention(qkv, dcum, dcum_t):
    t = qkv.shape[0]
    tq = FOX_Q_BLOCK
    width = FOX_PACK * FOX_DIM
    groups = FOX_HEADS // FOX_PACK
    return pl.pallas_call(
        _fox_kernel,
        grid=(groups, t // tq),
        in_specs=[pl.BlockSpec((tq, width), lambda h, i: (i, h)),
                  pl.BlockSpec((t, width), lambda h, i: (0, groups + h)),
                  pl.BlockSpec((t, width), lambda h, i: (0, 2 * groups + h)),
                  pl.BlockSpec((t, LANES), lambda h, i: (0, 0)),
                  pl.BlockSpec((FOX_PACK, 1, t), lambda h, i: (h, 0, 0))],
        out_specs=pl.BlockSpec((tq, width), lambda h, i: (i, h)),
        out_shape=jax.ShapeDtypeStruct((t, FOX_HEADS * FOX_DIM), BF16),
        compiler_params=_params(2),
        name="fox_attention",
    )(qkv, qkv, qkv, dcum, dcum_t)


def _swiglu_half(h, hn, norm_w, w_in, w_out, layer):
    n_main = (D_FF // FFN_TILE) * FFN_TILE
    n_tail = D_FF - n_main
    if hn is None:
        act_tail, hn, w_tail = ffn_in(h, w_in, layer, n_main, n_tail, n_tail, norm_w=norm_w, w_out=w_out)
    else:
        act_tail, w_tail = ffn_in(hn, w_in, layer, n_main, n_tail, n_tail, w_out=w_out)
    act_main, w_main = ffn_in(hn, w_in, layer, 0, n_main, FFN_TILE, w_out=w_out)
    return matmul_residual(h, None, 0, [(act_main, w_main), (act_tail, w_tail)], scale=0.5, tn=1024)


def _ssd_hgrn_layer(h, norm_w, w_in, layer, conv_w, conv_b, dt_bias, a_log, d_skip, ssd_norm_w, lb, hgrn_norm_w, w_out):
    d = D_MODEL
    nb = SSD_GROUPS * SSD_STATE
    o_dt = 2 * d + 2 * nb
    o_q = o_dt + SSD_HEADS
    tile = 1024 // SUBLANES
    wt = jnp.swapaxes(w_in, 1, 2)
    dt_raw, hn = matmul_nt(h, wt, layer, F32, LANES, lambda j: o_dt // SUBLANES, tn=LANES, norm_w=norm_w)
    proj = matmul_nt(hn, wt, layer, BF16, o_dt, lambda j: j * tile)
    first = o_q // SUBLANES
    qvg = matmul_nt(hn, wt, layer, BF16, 3 * d, lambda j: first + (j + jnp.where(j >= 2, 2, 0)) * tile)
    f_raw = matmul_nt(hn, wt, layer, F32, d, lambda j: first + (j + 2) * tile)
    pad_heads = lambda v: jnp.pad(v, (0, LANES - SSD_HEADS)).reshape(1, LANES)
    y_a = ssd_mixer(proj, dt_raw, conv_w[:, :d], conv_b[:d].reshape(1, d), conv_w[:, d:],
                    conv_b[d:].reshape(1, 2 * nb), pad_heads(dt_bias), pad_heads(a_log),
                    jnp.repeat(d_skip, SSD_HEAD_DIM).reshape(1, d), ssd_norm_w.reshape(1, d))
    y_b = hgrn_mixer(qvg, f_raw, lb.reshape(1, d), hgrn_norm_w.reshape(1, d))
    return matmul_residual(h, w_out, layer, [(y_a, 0), (y_b, d)], tn=1024, weight_buffers=1)


def _fox_layer(h, norm_w, w_in, layer, b_f, w_out):
    d = D_MODEL
    t = h.shape[0]
    tile = 1024 // SUBLANES
    wt = jnp.swapaxes(w_in, 1, 2)
    f_raw, hn = matmul_nt(h, wt, layer, F32, LANES, lambda j: (3 * d + FOX_HEADS - LANES) // SUBLANES, tn=LANES,
                          norm_w=norm_w)
    qkv = matmul_nt(hn, wt, layer, BF16, 3 * d, lambda j: j * tile, scaled_blocks=d // 1024,
                    scale=LOG2E * FOX_DIM ** -0.5)
    b_pad = jnp.pad(b_f, (FOX_F_LANE0, 0)).reshape(1, LANES)
    dcum, dcum_t = logf_cumsum(f_raw, b_pad)
    o = fox_attention(qkv, dcum, dcum_t[FOX_F_LANE0:].reshape(FOX_HEADS, 1, t))
    return matmul_residual(h, w_out, layer, [(o, 0)], tm=1024, tn=1024)


def kernel(x, p, ffn1_norm, ffn1_w_in, ffn1_w_out, mix_norm, ab_w_in, ssd_conv_w, ssd_conv_b, ssd_dt_bias,
           ssd_a_log, ssd_d, ssd_norm, hgrn_lb_logits, hgrn_norm, ab_w_out, fox_w_in, fox_b_f, fox_w_out,
           ffn2_norm, ffn2_w_in, ffn2_w_out, ple_gate_norm, ple_w_gate, ple_w_up, ple_norm, final_norm):
    bsz, t, d = x.shape
    depth = p.shape[0]
    assert bsz == 1 and d == D_MODEL
    lb_all = jnp.cumsum(jax.nn.softmax(hgrn_lb_logits.astype(F32), axis=0), axis=0)
    h = x.reshape(t, d)
    hn = None
    for i in range(depth):
        j = i // 2
        h = _swiglu_half(h, hn, ffn1_norm[i], ffn1_w_in, ffn1_w_out, i)
        if i % 2 == 0:
            h = _ssd_hgrn_layer(h, mix_norm[i], ab_w_in, j, ssd_conv_w[j], ssd_conv_b[j], ssd_dt_bias[j],
                                ssd_a_log[j], ssd_d[j], ssd_norm[j], lb_all[i], hgrn_norm[j], ab_w_out)
        else:
            h = _fox_layer(h, mix_norm[i], fox_w_in, j, fox_b_f[j], fox_w_out)
        h = _swiglu_half(h, None, ffn2_norm[i], ffn2_w_in, ffn2_w_out, i)
        final = i == depth - 1
        next_w = final_norm if final else ffn1_norm[i + 1]
        outs = ple_add(h, p, ple_w_gate, ple_w_up, i, ple_gate_norm[i], ple_norm[i], next_w, final)
        if final:
            return outs[0].reshape(bsz, t, d)
        h, hn = outs
```

```python
import functools
import math

import jax
import jax.numpy as jnp
from jax import lax
from jax.experimental import pallas as pl
from jax.experimental.pallas import tpu as pltpu

F32 = jnp.float32
BF16 = jnp.bfloat16
EPS = 1e-6

D_MODEL = 2048
D_FF = 5504
SSD_HEADS = 32
SSD_HEAD_DIM = 64
SSD_GROUPS = 4
SSD_STATE = 128
SSD_GROUP_WIDTH = 512
SSD_CONV = 4
HGRN_HEADS = 16
HGRN_DIM = 128
FOX_HEADS = 16
FOX_DIM = 128
LANES = 128
SUBLANES = 8
CONV_HALO = 16

SSD_ROWS = 128
HGRN_ROWS = 64
HGRN_CHUNKS_PER_STEP = 4
FOX_Q_BLOCK = 1024
FOX_K_BLOCK = 2048
FOX_PACK = 2
CUMSUM_ROWS = 128
FOX_F_LANE0 = LANES - FOX_HEADS
FFN_TILE = 512
VMEM_LIMIT = 56 * 1024 * 1024
LOG2E = math.log2(math.e)


def _params(n_axes, vmem=VMEM_LIMIT):
    return pltpu.CompilerParams(dimension_semantics=("arbitrary",) * n_axes, vmem_limit_bytes=vmem)


def _silu(x):
    return x * jax.nn.sigmoid(x)


def _softplus(x):
    return jnp.maximum(x, 0.0) + jnp.log(1.0 + jnp.exp(-jnp.abs(x)))


def _dot(a, b):
    return jnp.dot(a, b, preferred_element_type=F32)


def _dot_nt(a, b):
    return lax.dot_general(a, b, (((1,), (1,)), ((), ())), preferred_element_type=F32)


def _dot_tn(a, b):
    return lax.dot_general(a, b, (((0,), (0,)), ((), ())), preferred_element_type=F32)


def _dot_f32(a, b):
    return jnp.dot(a, b, preferred_element_type=F32, precision=lax.Precision.HIGHEST)


def _tril(n):
    r = lax.broadcasted_iota(jnp.int32, (n, n), 0)
    c = lax.broadcasted_iota(jnp.int32, (n, n), 1)
    return r >= c


def _weight_window(layer, rows, width, row_tile, col_tile, buffers=2):
    return pl.BlockSpec((pl.Element(1), pl.Element(rows), pl.Element(width)),
                        lambda j, i: (layer, row_tile(j) * SUBLANES, col_tile(j) * LANES),
                        pipeline_mode=pl.Buffered(buffers))


def _cast_weights_once(pairs):
    @pl.when(pl.program_id(1) == 0)
    def _():
        for w_ref, wb_ref in pairs:
            wb_ref[...] = w_ref[0].astype(BF16)


def _normed_rows(x_ref, nw_ref, hn_ref):
    x = x_ref[...]
    xn = (x * lax.rsqrt(jnp.mean(x * x, axis=-1, keepdims=True) + EPS) * nw_ref[...]).astype(BF16)
    hn_ref[...] = xn
    return xn


def _mm_nt_kernel(x_ref, w_ref, *refs, scaled_blocks, scale, normalize):
    wb_ref = refs[-1]
    _cast_weights_once([(w_ref, wb_ref)])
    if normalize:
        nw_ref, o_ref, hn_ref = refs[:3]
        x = _normed_rows(x_ref, nw_ref, hn_ref)
    else:
        o_ref = refs[0]
        x = x_ref[...]
    acc = _dot_nt(x, wb_ref[...])
    if scaled_blocks:
        acc = acc * jnp.where(pl.program_id(0) < scaled_blocks, scale, 1.0)
    o_ref[...] = acc.astype(o_ref.dtype)


def matmul_nt(x, wt, layer, out_dtype, n_out, row_tile, tn=1024, tm=1024, scaled_blocks=0, scale=1.0, norm_w=None):
    m, k = x.shape
    normalize = norm_w is not None
    assert not normalize or n_out == tn
    in_specs = [pl.BlockSpec((tm, k), lambda j, i: (i, 0)), _weight_window(layer, tn, k, row_tile, lambda j: 0)]
    out_specs = [pl.BlockSpec((tm, tn), lambda j, i: (i, j))]
    out_shape = [jax.ShapeDtypeStruct((m, n_out), out_dtype)]
    args = [x, wt]
    if normalize:
        in_specs.append(pl.BlockSpec((1, k), lambda j, i: (0, 0)))
        out_specs.append(pl.BlockSpec((tm, k), lambda j, i: (i, 0)))
        out_shape.append(jax.ShapeDtypeStruct((m, k), BF16))
        args.append(norm_w.reshape(1, k))
    outs = pl.pallas_call(
        functools.partial(_mm_nt_kernel, scaled_blocks=scaled_blocks, scale=scale, normalize=normalize),
        grid=(n_out // tn, m // tm),
        in_specs=in_specs,
        out_specs=out_specs,
        out_shape=out_shape,
        scratch_shapes=[pltpu.VMEM((tn, k), BF16)],
        compiler_params=_params(2),
        name="matmul_nt",
    )(*args)
    return outs if normalize else outs[0]


def _ffn_in_kernel(x_ref, wg_ref, wu_ref, *refs, normalize, cast_out_weight):
    refs = list(refs)
    nw_ref = refs.pop(0) if normalize else None
    wo_ref = refs.pop(0) if cast_out_weight else None
    o_ref = refs.pop(0)
    hn_ref = refs.pop(0) if normalize else None
    wob_ref = refs.pop(0) if cast_out_weight else None
    wgb_ref, wub_ref = refs
    _cast_weights_once([(wg_ref, wgb_ref), (wu_ref, wub_ref)])
    x = _normed_rows(x_ref, nw_ref, hn_ref) if normalize else x_ref[...]
    if cast_out_weight:
        wob_ref[...] = wo_ref[0].astype(BF16)
    gate = _dot(x, wgb_ref[...])
    up = _dot(x, wub_ref[...])
    o_ref[...] = (_silu(gate) * up).astype(o_ref.dtype)


def ffn_in(x, w_in, layer, col0, n_out, tn, tm=1024, norm_w=None, w_out=None):
    m, k = x.shape
    normalize = norm_w is not None
    cast_out_weight = w_out is not None
    assert not normalize or n_out == tn
    blocks_per_tile = tn // LANES
    gate_col = lambda j: col0 // LANES + j * blocks_per_tile
    up_col = lambda j: (D_FF + col0) // LANES + j * blocks_per_tile
    in_specs = [pl.BlockSpec((tm, k), lambda j, i: (i, 0)),
                _weight_window(layer, k, tn, lambda j: 0, gate_col),
                _weight_window(layer, k, tn, lambda j: 0, up_col)]
    out_specs = [pl.BlockSpec((tm, tn), lambda j, i: (i, j))]
    out_shape = [jax.ShapeDtypeStruct((m, n_out), BF16)]
    args = [x, w_in, w_in]
    if normalize:
        in_specs.append(pl.BlockSpec((1, k), lambda j, i: (0, 0)))
        out_specs.append(pl.BlockSpec((tm, k), lambda j, i: (i, 0)))
        out_shape.append(jax.ShapeDtypeStruct((m, k), BF16))
        args.append(norm_w.reshape(1, k))
    if cast_out_weight:
        cols_out = w_out.shape[2]
        steps_i = m // tm
        slab = n_out // ((n_out // tn) * steps_i)
        pack = 2 * SUBLANES
        assert slab % pack == 0 and col0 % pack == 0
        in_specs.append(pl.BlockSpec(
            (pl.Element(1), pl.Element(slab), pl.Element(cols_out)),
            lambda j, i: (layer, (col0 // pack + (j * steps_i + i) * (slab // pack)) * pack, 0)))
        out_specs.append(pl.BlockSpec((slab, cols_out), lambda j, i: (j * steps_i + i, 0)))
        out_shape.append(jax.ShapeDtypeStruct((n_out, cols_out), BF16))
        args.append(w_out)
    outs = pl.pallas_call(
        functools.partial(_ffn_in_kernel, normalize=normalize, cast_out_weight=cast_out_weight),
        grid=(n_out // tn, m // tm),
        in_specs=in_specs,
        out_specs=out_specs,
        out_shape=out_shape,
        scratch_shapes=[pltpu.VMEM((k, tn), BF16), pltpu.VMEM((k, tn), BF16)],
        compiler_params=_params(2),
        name="ffn_in",
    )(*args)
    return outs if normalize or cast_out_weight else outs[0]


def _mm_resid_kernel(*refs, n_pairs, scale, staged):
    h_ref = refs[0]
    a_refs = refs[1:1 + n_pairs]
    w_refs = refs[1 + n_pairs:1 + 2 * n_pairs]
    o_ref = refs[1 + 2 * n_pairs]
    if staged:
        wb_refs = refs[2 + 2 * n_pairs:]
        _cast_weights_once(list(zip(w_refs, wb_refs)))
        weights = [wb_ref[...] for wb_ref in wb_refs]
    else:
        weights = [w_ref[0] for w_ref in w_refs]
    acc = _dot(a_refs[0][...], weights[0])
    for a_ref, w in zip(a_refs[1:], weights[1:]):
        acc = acc + _dot(a_ref[...], w)
    if scale != 1.0:
        acc = scale * acc
    o_ref[...] = h_ref[...] + acc


def matmul_residual(h, w, layer, pieces, scale=1.0, tm=512, tn=512, weight_buffers=2):
    m, n = h.shape
    staged = w is not None
    if not staged:
        w_list = [wi.reshape(1, *wi.shape) for _, wi in pieces]
        pieces = [(a, 0) for a, _ in pieces]
    else:
        w_list = [w] * len(pieces)
    blocks_per_tile = tn // LANES
    a_specs = [pl.BlockSpec((tm, a.shape[1]), lambda j, i: (i, 0)) for a, _ in pieces]
    w_specs = [_weight_window(layer, a.shape[1], tn, lambda j, r=row0 // SUBLANES: r, lambda j: j * blocks_per_tile,
                              weight_buffers) for a, row0 in pieces]
    return pl.pallas_call(
        functools.partial(_mm_resid_kernel, n_pairs=len(pieces), scale=scale, staged=staged),
        grid=(n // tn, m // tm),
        in_specs=[pl.BlockSpec((tm, tn), lambda j, i: (i, j))] + a_specs + w_specs,
        out_specs=pl.BlockSpec((tm, tn), lambda j, i: (i, j)),
        out_shape=jax.ShapeDtypeStruct((m, n), F32),
        scratch_shapes=[pltpu.VMEM((a.shape[1], tn), BF16) for a, _ in pieces] if staged else [],
        compiler_params=_params(2),
        name="matmul_residual",
    )(h, *[a for a, _ in pieces], *w_list)


def _ple_kernel(h_ref, p_ref, wg_ref, wu_ref, gn_ref, pn_ref, nn_ref, *refs, final):
    wgb_ref, wub_ref = refs[-2:]
    out_refs = refs[:-2]

    @pl.when(pl.program_id(0) == 0)
    def _():
        wgb_ref[...] = wg_ref[...].astype(BF16)
        wub_ref[...] = wu_ref[...].astype(BF16)

    def normed(x, w_ref):
        return x * lax.rsqrt(jnp.mean(x * x, axis=-1, keepdims=True) + EPS) * w_ref[...]

    h = h_ref[...]
    gate = jax.nn.sigmoid(_dot(normed(h, gn_ref).astype(BF16), wgb_ref[...]))
    emb = normed(_dot(p_ref[...].astype(BF16), wub_ref[...]), pn_ref)
    h = h + emb * gate
    if not final:
        out_refs[0][...] = h
    out_refs[-1][...] = normed(h, nn_ref).astype(out_refs[-1].dtype)


def ple_add(h, p, w_gate, w_up, layer, gate_norm_w, post_norm_w, next_norm_w, final, tm=256):
    m, d = h.shape
    pd = p.shape[-1]
    row = lambda i: (i, 0)
    fixed = lambda i: (0, 0)
    resident = functools.partial(pl.BlockSpec, index_map=lambda i: (layer, 0, 0), pipeline_mode=pl.Buffered(1))
    out_dtypes = [F32] if final else [F32, BF16]
    return pl.pallas_call(
        functools.partial(_ple_kernel, final=final),
        grid=(m // tm,),
        in_specs=[pl.BlockSpec((tm, d), row),
                  pl.BlockSpec((None, None, tm, pd), lambda i: (layer, 0, i, 0)),
                  resident((None, d, d)), resident((None, pd, d)),
                  pl.BlockSpec((1, d), fixed), pl.BlockSpec((1, d), fixed), pl.BlockSpec((1, d), fixed)],
        out_specs=[pl.BlockSpec((tm, d), row) for _ in out_dtypes],
        out_shape=[jax.ShapeDtypeStruct((m, d), dt) for dt in out_dtypes],
        scratch_shapes=[pltpu.VMEM((d, d), BF16), pltpu.VMEM((pd, d), BF16)],
        compiler_params=_params(1),
        name="ple_add",
    )(h, p, w_gate, w_up, gate_norm_w.reshape(1, d), post_norm_w.reshape(1, d), next_norm_w.reshape(1, d))


def _ssd_kernel(z_ref, xs_ref, bc_ref, dt_ref, cwx_ref, cbx_ref, cwbc_ref, cbbc_ref, dtb_ref, alog_ref,
                dskip_ref, nw_ref, o_ref, xhalo_ref, bchalo_ref, st_ref, spread_ref):
    rows = SSD_ROWS

    @pl.when(pl.program_id(0) == 0)
    def _():
        xhalo_ref[...] = jnp.zeros(xhalo_ref.shape, BF16)
        bchalo_ref[...] = jnp.zeros(bchalo_ref.shape, BF16)
        st_ref[...] = jnp.zeros(st_ref.shape, F32)
        head = lax.broadcasted_iota(jnp.int32, spread_ref.shape, 0)
        lane = lax.broadcasted_iota(jnp.int32, spread_ref.shape, 1)
        spread_ref[...] = jnp.where(lane // SSD_HEAD_DIM == head, 1.0, 0.0).astype(BF16)

    out_row = lax.broadcasted_iota(jnp.int32, (rows, CONV_HALO + rows), 0)
    in_row = lax.broadcasted_iota(jnp.int32, (rows, CONV_HALO + rows), 1)
    shifts = [jnp.where(in_row == out_row + CONV_HALO - back, 1.0, 0.0).astype(BF16)
              for back in range(SSD_CONV - 1, 0, -1)]

    def conv_silu(halo_ref, raw_ref, w_ref, b_ref):
        raw = raw_ref[...]
        ext = jnp.concatenate([halo_ref[...], raw], axis=0)
        acc = b_ref[...] + raw.astype(F32) * w_ref[SSD_CONV - 1:SSD_CONV, :]
        for k, shift in enumerate(shifts):
            acc = acc + _dot(shift, ext) * w_ref[k:k + 1, :]
        halo_ref[...] = raw[rows - CONV_HALO:, :]
        return _silu(acc)

    xs = conv_silu(xhalo_ref, xs_ref, cwx_ref, cbx_ref)
    bc = conv_silu(bchalo_ref, bc_ref, cwbc_ref, cbbc_ref)

    is_head = lax.broadcasted_iota(jnp.int32, (rows, LANES), 1) < SSD_HEADS
    dt = jnp.where(is_head, _softplus(dt_ref[...] + dtb_ref[...]), 0.0)
    a = -jnp.exp(alog_ref[...])
    causal = _tril(rows)
    cum = _dot_f32(causal.astype(F32), dt * a)
    cum_t = cum.T
    low_half = lax.broadcasted_iota(jnp.int32, (rows, LANES), 1) < SSD_HEAD_DIM

    def per_lane(v, terms):
        out, rest = None, v
        for _ in range(terms):
            piece = rest.astype(BF16)
            rest = rest - piece.astype(F32)
            moved = _dot(piece, spread_ref[...])
            out = moved if out is None else out + moved
        return out

    dt_x = per_lane(dt, 2)
    cum_x = per_lane(cum, 3)

    pairs_per_group = SSD_GROUP_WIDTH // LANES
    for g in range(SSD_GROUPS):
        b_g = bc[:, g * SSD_STATE:(g + 1) * SSD_STATE]
        c_g = bc[:, (SSD_GROUPS + g) * SSD_STATE:(SSD_GROUPS + g + 1) * SSD_STATE].astype(BF16)
        cb = _dot_nt(c_g, b_g.astype(BF16))
        st = st_ref[g]
        y_off = _dot(c_g, st.astype(BF16))
        y_tiles, xw_tiles, dec_tiles = [], [], []
        for jj in range(pairs_per_group):
            j = g * pairs_per_group + jj
            h0 = 2 * j
            sl = slice(j * LANES, (j + 1) * LANES)
            x_p = xs[:, sl]
            cum_p = cum_x[:, sl]
            xdt = x_p * dt_x[:, sl]
            xdt_b = xdt.astype(BF16)
            halves = []
            for h in (h0, h0 + 1):
                diff = cum[:, h:h + 1] - cum_t[h:h + 1, :]
                decay = jnp.exp(jnp.where(causal, diff, -jnp.inf))
                halves.append(_dot((cb * decay).astype(BF16), xdt_b))
            y_diag = jnp.where(low_half, halves[0], halves[1])
            cum_last = cum_p[rows - 1:rows, :]
            xw_tiles.append((xdt * jnp.exp(cum_last - cum_p)).astype(BF16))
            dec_tiles.append(jnp.exp(cum_last))
            y = y_diag + y_off[:, jj * LANES:(jj + 1) * LANES] * jnp.exp(cum_p) + x_p * dskip_ref[:, sl]
            y_tiles.append(y * _silu(z_ref[:, sl].astype(F32)))
        xw = jnp.concatenate(xw_tiles, axis=1)
        st_ref[g] = st * jnp.concatenate(dec_tiles, axis=1) + _dot(b_g.T.astype(BF16), xw)
        ss = jnp.sum(y_tiles[0] * y_tiles[0], axis=-1, keepdims=True)
        for t in y_tiles[1:]:
            ss = ss + jnp.sum(t * t, axis=-1, keepdims=True)
        inv = lax.rsqrt(ss * (1.0 / SSD_GROUP_WIDTH) + EPS)
        for jj in range(pairs_per_group):
            sl = slice((g * pairs_per_group + jj) * LANES, (g * pairs_per_group + jj + 1) * LANES)
            o_ref[:, sl] = (y_tiles[jj] * inv * nw_ref[:, sl]).astype(o_ref.dtype)


def ssd_mixer(proj, dt_raw, conv_wx, conv_bx, conv_wbc, conv_bbc, dt_bias, a_log, d_skip_x, norm_w):
    t = proj.shape[0]
    rows = SSD_ROWS
    d = D_MODEL
    nbc = 2 * SSD_GROUPS * SSD_STATE
    fixed = lambda i: (0, 0)
    return pl.pallas_call(
        _ssd_kernel,
        grid=(t // rows,),
        in_specs=[pl.BlockSpec((rows, d), lambda i: (i, 0)),
                  pl.BlockSpec((rows, d), lambda i: (i, 1)),
                  pl.BlockSpec((rows, nbc), lambda i: (i, 2 * d // nbc)),
                  pl.BlockSpec((rows, LANES), lambda i: (i, 0)),
                  pl.BlockSpec((SSD_CONV, d), fixed), pl.BlockSpec((1, d), fixed),
                  pl.BlockSpec((SSD_CONV, nbc), fixed), pl.BlockSpec((1, nbc), fixed),
                  pl.BlockSpec((1, LANES), fixed), pl.BlockSpec((1, LANES), fixed),
                  pl.BlockSpec((1, d), fixed), pl.BlockSpec((1, d), fixed)],
        out_specs=pl.BlockSpec((rows, d), lambda i: (i, 0)),
        out_shape=jax.ShapeDtypeStruct((t, d), BF16),
        scratch_shapes=[pltpu.VMEM((CONV_HALO, d), BF16),
                        pltpu.VMEM((CONV_HALO, nbc), BF16),
                        pltpu.VMEM((SSD_GROUPS, SSD_STATE, SSD_GROUP_WIDTH), F32),
                        pltpu.VMEM((LANES, d), BF16)],
        compiler_params=_params(1),
        name="ssd_mixer",
    )(proj, proj, proj, dt_raw, conv_wx, conv_bx, conv_wbc, conv_bbc, dt_bias, a_log, d_skip_x, norm_w)


def _hgrn_kernel(q_ref, v_ref, g_ref, f_ref, lb_ref, nw_ref, o_ref, st_ref):
    rows = HGRN_ROWS

    @pl.when(pl.program_id(0) == 0)
    def _():
        st_ref[...] = jnp.zeros(st_ref.shape, F32)

    lb = lb_ref[...]
    causal = _tril(rows)
    tri = causal.astype(F32)
    for c in range(HGRN_CHUNKS_PER_STEP):
        r = slice(c * rows, (c + 1) * rows)
        f = lb + (1.0 - lb) * jax.nn.sigmoid(f_ref[r, :])
        k = 1.0 - f
        cum = _dot_f32(tri, jnp.log(f))
        qf = _silu(q_ref[r, :].astype(F32))
        mid = cum[rows // 2 - 1:rows // 2, :]
        last = cum[rows - 1:rows, :]
        q_rel = (qf * jnp.exp(cum - mid)).astype(BF16)
        k_rel = (k * jnp.exp(mid - cum)).astype(BF16)
        k_end = (k * jnp.exp(last - cum)).astype(BF16)
        q_dec = (qf * jnp.exp(cum)).astype(BF16)
        chunk_decay = jnp.exp(last)
        for h in range(HGRN_HEADS):
            sl = slice(h * HGRN_DIM, (h + 1) * HGRN_DIM)
            v_h = v_ref[r, sl]
            att = jnp.where(causal, _dot_nt(q_rel[:, sl], k_rel[:, sl]), 0.0)
            st = st_ref[h]
            o = _dot(att.astype(BF16), v_h) + _dot_nt(q_dec[:, sl], st.astype(BF16))
            st_ref[h] = st * chunk_decay[:, sl] + _dot_tn(v_h, k_end[:, sl])
            on = o * lax.rsqrt(jnp.mean(o * o, axis=-1, keepdims=True) + EPS) * nw_ref[:, sl]
            o_ref[r, sl] = (on * _silu(g_ref[r, sl].astype(F32))).astype(o_ref.dtype)


def hgrn_mixer(qvg, f_raw, lb, norm_w):
    t = qvg.shape[0]
    rows = HGRN_ROWS * HGRN_CHUNKS_PER_STEP
    d = D_MODEL
    fixed = lambda i: (0, 0)
    return pl.pallas_call(
        _hgrn_kernel,
        grid=(t // rows,),
        in_specs=[pl.BlockSpec((rows, d), lambda i: (i, 0)),
                  pl.BlockSpec((rows, d), lambda i: (i, 1)),
                  pl.BlockSpec((rows, d), lambda i: (i, 2)),
                  pl.BlockSpec((rows, d), lambda i: (i, 0)),
                  pl.BlockSpec((1, d), fixed), pl.BlockSpec((1, d), fixed)],
        out_specs=pl.BlockSpec((rows, d), lambda i: (i, 0)),
        out_shape=jax.ShapeDtypeStruct((t, d), BF16),
        scratch_shapes=[pltpu.VMEM((HGRN_HEADS, HGRN_DIM, HGRN_DIM), F32)],
        compiler_params=_params(1),
        name="hgrn_mixer",
    )(qvg, qvg, qvg, f_raw, lb, norm_w)


def _forget_gate_kernel(x_ref, w_ref, nw_ref, b_ref, c_ref, ct_ref, hn_ref, wb_ref, carry_ref):
    rows = x_ref.shape[0]

    @pl.when(pl.program_id(1) == 0)
    def _():
        carry_ref[...] = jnp.zeros(carry_ref.shape, F32)

    _cast_weights_once([(w_ref, wb_ref)])
    f_raw = _dot_nt(_normed_rows(x_ref, nw_ref, hn_ref), wb_ref[...])
    log_f = -_softplus(-(f_raw + b_ref[...]))
    tri = _tril(CUMSUM_ROWS).astype(F32)
    carry = carry_ref[...]
    for s in range(rows // CUMSUM_ROWS):
        r = slice(s * CUMSUM_ROWS, (s + 1) * CUMSUM_ROWS)
        c = _dot_f32(tri, log_f[r]) + carry
        carry = c[CUMSUM_ROWS - 1:CUMSUM_ROWS, :]
        c2 = c * LOG2E
        c_ref[r, :] = c2
        ct_ref[:, r] = c2.T
    carry_ref[...] = carry


def forget_gate_cumsum(h, norm_w, wt, layer, row0, b_f, tm=1024):
    t, k = h.shape
    return pl.pallas_call(
        _forget_gate_kernel,
        grid=(1, t // tm),
        in_specs=[pl.BlockSpec((tm, k), lambda j, i: (i, 0)),
                  _weight_window(layer, LANES, k, lambda j: row0 // SUBLANES, lambda j: 0),
                  pl.BlockSpec((1, k), lambda j, i: (0, 0)), pl.BlockSpec((1, LANES), lambda j, i: (0, 0))],
        out_specs=[pl.BlockSpec((tm, LANES), lambda j, i: (i, 0)), pl.BlockSpec((LANES, tm), lambda j, i: (0, i)),
                   pl.BlockSpec((tm, k), lambda j, i: (i, 0))],
        out_shape=[jax.ShapeDtypeStruct((t, LANES), F32), jax.ShapeDtypeStruct((LANES, t), F32),
                   jax.ShapeDtypeStruct((t, k), BF16)],
        scratch_shapes=[pltpu.VMEM((LANES, k), BF16), pltpu.VMEM((1, LANES), F32)],
        compiler_params=_params(2),
        name="forget_gate_cumsum",
    )(h, wt, norm_w.reshape(1, k), b_f)


def _fox_kernel(q_ref, k_ref, v_ref, dkc_ref, dqr_ref, o_ref):
    tq, tk = FOX_Q_BLOCK, FOX_K_BLOCK
    half = tq // 2
    hp = pl.program_id(0)
    qi = pl.program_id(1)
    q_start = pl.multiple_of(qi * tq, tq)
    heads = []
    for hh in range(FOX_PACK):
        sl = slice(hh * FOX_DIM, (hh + 1) * FOX_DIM)
        heads.append((hh, sl, q_ref[:, sl], dqr_ref[hh, :, pl.ds(q_start, tq)]))

    def step(start, width, lo, carry, diagonal):
        out = []
        lane = lax.broadcasted_iota(jnp.int32, (width, LANES), 1)
        dk_all = dkc_ref[pl.ds(start, width), :]
        for (hh, sl, q, dq), (m, l, acc) in zip(heads, carry):
            k_b = k_ref[pl.ds(start, width), sl]
            v_b = v_ref[pl.ds(start, width), sl]
            dk = jnp.sum(jnp.where(lane == FOX_F_LANE0 + hp * FOX_PACK + hh, dk_all, 0.0), axis=-1, keepdims=True)
            s = _dot_nt(k_b, q[lo:]) - dk
            if diagonal:
                key = lax.broadcasted_iota(jnp.int32, s.shape, 0)
                qry = lax.broadcasted_iota(jnp.int32, s.shape, 1)
                s = jnp.where(qry >= key, s, -jnp.inf)
            m_new = jnp.maximum(m, jnp.max(s, axis=0, keepdims=True) + dq[:, lo:])
            alpha = jnp.exp2(m - m_new)
            p = jnp.exp2(s - (m_new - dq[:, lo:]))
            l = alpha * l + jnp.sum(p, axis=0, keepdims=True)
            acc = alpha * acc + _dot_tn(v_b, p.astype(BF16))
            out.append((m_new, l, acc))
        return tuple(out)

    init = tuple((jnp.full((1, tq), -jnp.inf, F32), jnp.zeros((1, tq), F32), jnp.zeros((FOX_DIM, tq), F32))
                 for _ in heads)
    n_wide = lax.shift_right_logical(qi, 1)
    carry = lax.fori_loop(0, n_wide, lambda ki, c: step(pl.multiple_of(ki * tk, tk), tk, 0, c, False), init)
    odd_start = pl.multiple_of(n_wide * tk, tk)
    carry = lax.fori_loop(0, qi & 1, lambda _, c: step(odd_start, tq, 0, c, False), carry)
    carry = step(q_start, half, 0, carry, True)
    hi = step(pl.multiple_of(q_start + half, half), half, half,
              tuple((m[:, half:], l[:, half:], acc[:, half:]) for m, l, acc in carry), True)
    for (hh, sl, _, _), (m, l, acc), (m_hi, l_hi, acc_hi) in zip(heads, carry, hi):
        l = jnp.concatenate([l[:, :half], l_hi], axis=1)
        acc = jnp.concatenate([acc[:, :half], acc_hi], axis=1)
        o_ref[:, sl] = (acc / l).T.astype(o_ref.dtype)


def fox_attention(qkv, dcum, dcum_t):
    t = qkv.shape[0]
    tq = FOX_Q_BLOCK
    width = FOX_PACK * FOX_DIM
    groups = FOX_HEADS // FOX_PACK
    return pl.pallas_call(
        _fox_kernel,
        grid=(groups, t // tq),
        in_specs=[pl.BlockSpec((tq, width), lambda h, i: (i, h)),
                  pl.BlockSpec((t, width), lambda h, i: (0, groups + h)),
                  pl.BlockSpec((t, width), lambda h, i: (0, 2 * groups + h)),
                  pl.BlockSpec((t, LANES), lambda h, i: (0, 0)),
                  pl.BlockSpec((FOX_PACK, 1, t), lambda h, i: (h, 0, 0))],
        out_specs=pl.BlockSpec((tq, width), lambda h, i: (i, h)),
        out_shape=jax.ShapeDtypeStruct((t, FOX_HEADS * FOX_DIM), BF16),
        compiler_params=_params(2),
        name="fox_attention",
    )(qkv, qkv, qkv, dcum, dcum_t)


def _swiglu_half(h, hn, norm_w, w_in, w_out, layer):
    n_main = (D_FF // FFN_TILE) * FFN_TILE
    n_tail = D_FF - n_main
    if hn is None:
        act_tail, hn, w_tail = ffn_in(h, w_in, layer, n_main, n_tail, n_tail, norm_w=norm_w, w_out=w_out)
    else:
        act_tail, w_tail = ffn_in(hn, w_in, layer, n_main, n_tail, n_tail, w_out=w_out)
    act_main, w_main = ffn_in(hn, w_in, layer, 0, n_main, FFN_TILE, w_out=w_out)
    return matmul_residual(h, None, 0, [(act_main, w_main), (act_tail, w_tail)], scale=0.5, tn=1024)


def _ssd_hgrn_layer(h, norm_w, w_in, layer, conv_w, conv_b, dt_bias, a_log, d_skip, ssd_norm_w, lb, hgrn_norm_w, w_out):
    d = D_MODEL
    nb = SSD_GROUPS * SSD_STATE
    o_dt = 2 * d + 2 * nb
    o_q = o_dt + SSD_HEADS
    tile = 1024 // SUBLANES
    wt = jnp.swapaxes(w_in, 1, 2)
    dt_raw, hn = matmul_nt(h, wt, layer, F32, LANES, lambda j: o_dt // SUBLANES, tn=LANES, norm_w=norm_w)
    proj = matmul_nt(hn, wt, layer, BF16, o_dt, lambda j: j * tile)
    first = o_q // SUBLANES
    qvg = matmul_nt(hn, wt, layer, BF16, 3 * d, lambda j: first + (j + jnp.where(j >= 2, 2, 0)) * tile)
    f_raw = matmul_nt(hn, wt, layer, F32, d, lambda j: first + (j + 2) * tile)
    pad_heads = lambda v: jnp.pad(v, (0, LANES - SSD_HEADS)).reshape(1, LANES)
    y_a = ssd_mixer(proj, dt_raw, conv_w[:, :d], conv_b[:d].reshape(1, d), conv_w[:, d:],
                    conv_b[d:].reshape(1, 2 * nb), pad_heads(dt_bias), pad_heads(a_log),
                    jnp.repeat(d_skip, SSD_HEAD_DIM).reshape(1, d), ssd_norm_w.reshape(1, d))
    y_b = hgrn_mixer(qvg, f_raw, lb.reshape(1, d), hgrn_norm_w.reshape(1, d))
    return matmul_residual(h, w_out, layer, [(y_a, 0), (y_b, d)], tn=1024, weight_buffers=1)


def _fox_layer(h, norm_w, w_in, layer, b_f, w_out):
    d = D_MODEL
    t = h.shape[0]
    tile = 1024 // SUBLANES
    wt = jnp.swapaxes(w_in, 1, 2)
    b_pad = jnp.pad(b_f, (FOX_F_LANE0, 0)).reshape(1, LANES)
    dcum, dcum_t, hn = forget_gate_cumsum(h, norm_w, wt, layer, 3 * d + FOX_HEADS - LANES, b_pad)
    qkv = matmul_nt(hn, wt, layer, BF16, 3 * d, lambda j: j * tile, scaled_blocks=d // 1024,
                    scale=LOG2E * FOX_DIM ** -0.5)
    o = fox_attention(qkv, dcum, dcum_t[FOX_F_LANE0:].reshape(FOX_HEADS, 1, t))
    return matmul_residual(h, w_out, layer, [(o, 0)], tm=1024, tn=1024)


def kernel(x, p, ffn1_norm, ffn1_w_in, ffn1_w_out, mix_norm, ab_w_in, ssd_conv_w, ssd_conv_b, ssd_dt_bias,
           ssd_a_log, ssd_d, ssd_norm, hgrn_lb_logits, hgrn_norm, ab_w_out, fox_w_in, fox_b_f, fox_w_out,
           ffn2_norm, ffn2_w_in, ffn2_w_out, ple_gate_norm, ple_w_gate, ple_w_up, ple_norm, final_norm):
    bsz, t, d = x.shape
    depth = p.shape[0]
    assert bsz == 1 and d == D_MODEL
    lb_all = jnp.cumsum(jax.nn.softmax(hgrn_lb_logits.astype(F32), axis=0), axis=0)
    h = x.reshape(t, d)
    hn = None
    for i in range(depth):
        j = i // 2
        h = _swiglu_half(h, hn, ffn1_norm[i], ffn1_w_in, ffn1_w_out, i)
        if i % 2 == 0:
            h = _ssd_hgrn_layer(h, mix_norm[i], ab_w_in, j, ssd_conv_w[j], ssd_conv_b[j], ssd_dt_bias[j],
                                ssd_a_log[j], ssd_d[j], ssd_norm[j], lb_all[i], hgrn_norm[j], ab_w_out)
        else:
            h = _fox_layer(h, mix_norm[i], fox_w_in, j, fox_b_f[j], fox_w_out)
        h = _swiglu_half(h, None, ffn2_norm[i], ffn2_w_in, ffn2_w_out, i)
        final = i == depth - 1
        next_w = final_norm if final else ffn1_norm[i + 1]
        outs = ple_add(h, p, ple_w_gate, ple_w_up, i, ple_gate_norm[i], ple_norm[i], next_w, final)
        if final:
            return outs[0].reshape(bsz, t, d)
        h, hn = outs
```

```python
import functools
import math

import jax
import jax.numpy as jnp
from jax import lax
from jax.experimental import pallas as pl
from jax.experimental.pallas import tpu as pltpu

F32 = jnp.float32
BF16 = jnp.bfloat16
EPS = 1e-6

D_MODEL = 2048
D_FF = 5504
SSD_HEADS = 32
SSD_HEAD_DIM = 64
SSD_GROUPS = 4
SSD_STATE = 128
SSD_GROUP_WIDTH = 512
SSD_CONV = 4
HGRN_HEADS = 16
HGRN_DIM = 128
FOX_HEADS = 16
FOX_DIM = 128
LANES = 128
SUBLANES = 8
CONV_HALO = 16

SSD_ROWS = 128
HGRN_ROWS = 64
HGRN_CHUNKS_PER_STEP = 4
FOX_Q_BLOCK = 1024
FOX_K_BLOCK = 2048
FOX_PACK = 2
CUMSUM_ROWS = 128
FOX_F_LANE0 = LANES - FOX_HEADS
FFN_TILE = 512
VMEM_LIMIT = 56 * 1024 * 1024
LOG2E = math.log2(math.e)


def _params(n_axes, vmem=VMEM_LIMIT):
    return pltpu.CompilerParams(dimension_semantics=("arbitrary",) * n_axes, vmem_limit_bytes=vmem)


def _silu(x):
    return x * jax.nn.sigmoid(x)


def _softplus(x):
    return jnp.maximum(x, 0.0) + jnp.log(1.0 + jnp.exp(-jnp.abs(x)))


def _dot(a, b):
    return jnp.dot(a, b, preferred_element_type=F32)


def _dot_nt(a, b):
    return lax.dot_general(a, b, (((1,), (1,)), ((), ())), preferred_element_type=F32)


def _dot_tn(a, b):
    return lax.dot_general(a, b, (((0,), (0,)), ((), ())), preferred_element_type=F32)


def _dot_f32(a, b):
    return jnp.dot(a, b, preferred_element_type=F32, precision=lax.Precision.HIGHEST)


def _tril(n):
    r = lax.broadcasted_iota(jnp.int32, (n, n), 0)
    c = lax.broadcasted_iota(jnp.int32, (n, n), 1)
    return r >= c


def _weight_window(layer, rows, width, row_tile, col_tile, buffers=2):
    return pl.BlockSpec((pl.Element(1), pl.Element(rows), pl.Element(width)),
                        lambda j, i: (layer, row_tile(j) * SUBLANES, col_tile(j) * LANES),
                        pipeline_mode=pl.Buffered(buffers))


def _cast_weights_once(pairs):
    @pl.when(pl.program_id(1) == 0)
    def _():
        for w_ref, wb_ref in pairs:
            wb_ref[...] = w_ref[0].astype(BF16)


def _normed_rows(x_ref, nw_ref, hn_ref):
    x = x_ref[...]
    xn = (x * lax.rsqrt(jnp.mean(x * x, axis=-1, keepdims=True) + EPS) * nw_ref[...]).astype(BF16)
    hn_ref[...] = xn
    return xn


def _mm_nt_kernel(x_ref, w_ref, *refs, scaled_blocks, scale, normalize):
    wb_ref = refs[-1]
    _cast_weights_once([(w_ref, wb_ref)])
    if normalize:
        nw_ref, o_ref, hn_ref = refs[:3]
        x = _normed_rows(x_ref, nw_ref, hn_ref)
    else:
        o_ref = refs[0]
        x = x_ref[...]
    acc = _dot_nt(x, wb_ref[...])
    if scaled_blocks:
        acc = acc * jnp.where(pl.program_id(0) < scaled_blocks, scale, 1.0)
    o_ref[...] = acc.astype(o_ref.dtype)


def matmul_nt(x, wt, layer, out_dtype, n_out, row_tile, tn=1024, tm=1024, scaled_blocks=0, scale=1.0, norm_w=None):
    m, k = x.shape
    normalize = norm_w is not None
    assert not normalize or n_out == tn
    in_specs = [pl.BlockSpec((tm, k), lambda j, i: (i, 0)), _weight_window(layer, tn, k, row_tile, lambda j: 0)]
    out_specs = [pl.BlockSpec((tm, tn), lambda j, i: (i, j))]
    out_shape = [jax.ShapeDtypeStruct((m, n_out), out_dtype)]
    args = [x, wt]
    if normalize:
        in_specs.append(pl.BlockSpec((1, k), lambda j, i: (0, 0)))
        out_specs.append(pl.BlockSpec((tm, k), lambda j, i: (i, 0)))
        out_shape.append(jax.ShapeDtypeStruct((m, k), BF16))
        args.append(norm_w.reshape(1, k))
    outs = pl.pallas_call(
        functools.partial(_mm_nt_kernel, scaled_blocks=scaled_blocks, scale=scale, normalize=normalize),
        grid=(n_out // tn, m // tm),
        in_specs=in_specs,
        out_specs=out_specs,
        out_shape=out_shape,
        scratch_shapes=[pltpu.VMEM((tn, k), BF16)],
        compiler_params=_params(2),
        name="matmul_nt",
    )(*args)
    return outs if normalize else outs[0]


def _ffn_in_kernel(x_ref, wg_ref, wu_ref, *refs, normalize, cast_out_weight):
    refs = list(refs)
    nw_ref = refs.pop(0) if normalize else None
    wo_ref = refs.pop(0) if cast_out_weight else None
    o_ref = refs.pop(0)
    hn_ref = refs.pop(0) if normalize else None
    wob_ref = refs.pop(0) if cast_out_weight else None
    wgb_ref, wub_ref = refs
    _cast_weights_once([(wg_ref, wgb_ref), (wu_ref, wub_ref)])
    x = _normed_rows(x_ref, nw_ref, hn_ref) if normalize else x_ref[...]
    if cast_out_weight:
        wob_ref[...] = wo_ref[0].astype(BF16)
    gate = _dot(x, wgb_ref[...])
    up = _dot(x, wub_ref[...])
    o_ref[...] = (_silu(gate) * up).astype(o_ref.dtype)


def ffn_in(x, w_in, layer, col0, n_out, tn, tm=1024, norm_w=None, w_out=None):
    m, k = x.shape
    normalize = norm_w is not None
    cast_out_weight = w_out is not None
    assert not normalize or n_out == tn
    blocks_per_tile = tn // LANES
    gate_col = lambda j: col0 // LANES + j * blocks_per_tile
    up_col = lambda j: (D_FF + col0) // LANES + j * blocks_per_tile
    in_specs = [pl.BlockSpec((tm, k), lambda j, i: (i, 0)),
                _weight_window(layer, k, tn, lambda j: 0, gate_col),
                _weight_window(layer, k, tn, lambda j: 0, up_col)]
    out_specs = [pl.BlockSpec((tm, tn), lambda j, i: (i, j))]
    out_shape = [jax.ShapeDtypeStruct((m, n_out), BF16)]
    args = [x, w_in, w_in]
    if normalize:
        in_specs.append(pl.BlockSpec((1, k), lambda j, i: (0, 0)))
        out_specs.append(pl.BlockSpec((tm, k), lambda j, i: (i, 0)))
        out_shape.append(jax.ShapeDtypeStruct((m, k), BF16))
        args.append(norm_w.reshape(1, k))
    if cast_out_weight:
        cols_out = w_out.shape[2]
        steps_i = m // tm
        slab = n_out // ((n_out // tn) * steps_i)
        pack = 2 * SUBLANES
        assert slab % pack == 0 and col0 % pack == 0
        in_specs.append(pl.BlockSpec(
            (pl.Element(1), pl.Element(slab), pl.Element(cols_out)),
            lambda j, i: (layer, (col0 // pack + (j * steps_i + i) * (slab // pack)) * pack, 0)))
        out_specs.append(pl.BlockSpec((slab, cols_out), lambda j, i: (j * steps_i + i, 0)))
        out_shape.append(jax.ShapeDtypeStruct((n_out, cols_out), BF16))
        args.append(w_out)
    outs = pl.pallas_call(
        functools.partial(_ffn_in_kernel, normalize=normalize, cast_out_weight=cast_out_weight),
        grid=(n_out // tn, m // tm),
        in_specs=in_specs,
        out_specs=out_specs,
        out_shape=out_shape,
        scratch_shapes=[pltpu.VMEM((k, tn), BF16), pltpu.VMEM((k, tn), BF16)],
        compiler_params=_params(2),
        name="ffn_in",
    )(*args)
    return outs if normalize or cast_out_weight else outs[0]


def _mm_resid_kernel(*refs, n_pairs, scale, staged):
    h_ref = refs[0]
    a_refs = refs[1:1 + n_pairs]
    w_refs = refs[1 + n_pairs:1 + 2 * n_pairs]
    o_ref = refs[1 + 2 * n_pairs]
    if staged:
        wb_refs = refs[2 + 2 * n_pairs:]
        _cast_weights_once(list(zip(w_refs, wb_refs)))
        weights = [wb_ref[...] for wb_ref in wb_refs]
    else:
        weights = [w_ref[0] for w_ref in w_refs]
    acc = _dot(a_refs[0][...], weights[0])
    for a_ref, w in zip(a_refs[1:], weights[1:]):
        acc = acc + _dot(a_ref[...], w)
    if scale != 1.0:
        acc = scale * acc
    o_ref[...] = h_ref[...] + acc


def matmul_residual(h, w, layer, pieces, scale=1.0, tm=512, tn=512, weight_buffers=2):
    m, n = h.shape
    staged = w is not None
    if not staged:
        w_list = [wi.reshape(1, *wi.shape) for _, wi in pieces]
        pieces = [(a, 0) for a, _ in pieces]
    else:
        w_list = [w] * len(pieces)
    blocks_per_tile = tn // LANES
    a_specs = [pl.BlockSpec((tm, a.shape[1]), lambda j, i: (i, 0)) for a, _ in pieces]
    w_specs = [_weight_window(layer, a.shape[1], tn, lambda j, r=row0 // SUBLANES: r, lambda j: j * blocks_per_tile,
                              weight_buffers) for a, row0 in pieces]
    return pl.pallas_call(
        functools.partial(_mm_resid_kernel, n_pairs=len(pieces), scale=scale, staged=staged),
        grid=(n // tn, m // tm),
        in_specs=[pl.BlockSpec((tm, tn), lambda j, i: (i, j))] + a_specs + w_specs,
        out_specs=pl.BlockSpec((tm, tn), lambda j, i: (i, j)),
        out_shape=jax.ShapeDtypeStruct((m, n), F32),
        scratch_shapes=[pltpu.VMEM((a.shape[1], tn), BF16) for a, _ in pieces] if staged else [],
        compiler_params=_params(2),
        name="matmul_residual",
    )(h, *[a for a, _ in pieces], *w_list)


def _ple_kernel(h_ref, p_ref, wg_ref, wu_ref, gn_ref, pn_ref, nn_ref, *refs, final):
    wgb_ref, wub_ref = refs[-2:]
    out_refs = refs[:-2]

    @pl.when(pl.program_id(0) == 0)
    def _():
        wgb_ref[...] = wg_ref[...].astype(BF16)
        wub_ref[...] = wu_ref[...].astype(BF16)

    def normed(x, w_ref):
        return x * lax.rsqrt(jnp.mean(x * x, axis=-1, keepdims=True) + EPS) * w_ref[...]

    h = h_ref[...]
    gate = jax.nn.sigmoid(_dot(normed(h, gn_ref).astype(BF16), wgb_ref[...]))
    emb = normed(_dot(p_ref[...].astype(BF16), wub_ref[...]), pn_ref)
    h = h + emb * gate
    if not final:
        out_refs[0][...] = h
    out_refs[-1][...] = normed(h, nn_ref).astype(out_refs[-1].dtype)


def ple_add(h, p, w_gate, w_up, layer, gate_norm_w, post_norm_w, next_norm_w, final, tm=256):
    m, d = h.shape
    pd = p.shape[-1]
    row = lambda i: (i, 0)
    fixed = lambda i: (0, 0)
    resident = functools.partial(pl.BlockSpec, index_map=lambda i: (layer, 0, 0), pipeline_mode=pl.Buffered(1))
    out_dtypes = [F32] if final else [F32, BF16]
    return pl.pallas_call(
        functools.partial(_ple_kernel, final=final),
        grid=(m // tm,),
        in_specs=[pl.BlockSpec((tm, d), row),
                  pl.BlockSpec((None, None, tm, pd), lambda i: (layer, 0, i, 0)),
                  resident((None, d, d)), resident((None, pd, d)),
                  pl.BlockSpec((1, d), fixed), pl.BlockSpec((1, d), fixed), pl.BlockSpec((1, d), fixed)],
        out_specs=[pl.BlockSpec((tm, d), row) for _ in out_dtypes],
        out_shape=[jax.ShapeDtypeStruct((m, d), dt) for dt in out_dtypes],
        scratch_shapes=[pltpu.VMEM((d, d), BF16), pltpu.VMEM((pd, d), BF16)],
        compiler_params=_params(1),
        name="ple_add",
    )(h, p, w_gate, w_up, gate_norm_w.reshape(1, d), post_norm_w.reshape(1, d), next_norm_w.reshape(1, d))


def _ssd_dt_kernel(x_ref, w_ref, nw_ref, dtb_ref, alog_ref, dt_ref, cum_ref, hn_ref, wb_ref):
    rows = x_ref.shape[0]
    _cast_weights_once([(w_ref, wb_ref)])
    raw = _dot_nt(_normed_rows(x_ref, nw_ref, hn_ref), wb_ref[...])
    is_head = lax.broadcasted_iota(jnp.int32, (rows, LANES), 1) < SSD_HEADS
    dt = jnp.where(is_head, _softplus(raw + dtb_ref[...]), 0.0)
    dt_ref[...] = dt
    dta = dt * -jnp.exp(alog_ref[...])
    tri = _tril(SSD_ROWS).astype(F32)
    for s in range(rows // SSD_ROWS):
        r = slice(s * SSD_ROWS, (s + 1) * SSD_ROWS)
        cum_ref[r, :] = _dot_f32(tri, dta[r])


def ssd_dt(h, norm_w, wt, layer, row0, dt_bias, a_log, tm=1024):
    t, k = h.shape
    fixed = lambda j, i: (0, 0)
    row = lambda j, i: (i, 0)
    return pl.pallas_call(
        _ssd_dt_kernel,
        grid=(1, t // tm),
        in_specs=[pl.BlockSpec((tm, k), row),
                  _weight_window(layer, LANES, k, lambda j: row0 // SUBLANES, lambda j: 0),
                  pl.BlockSpec((1, k), fixed), pl.BlockSpec((1, LANES), fixed), pl.BlockSpec((1, LANES), fixed)],
        out_specs=[pl.BlockSpec((tm, LANES), row), pl.BlockSpec((tm, LANES), row), pl.BlockSpec((tm, k), row)],
        out_shape=[jax.ShapeDtypeStruct((t, LANES), F32), jax.ShapeDtypeStruct((t, LANES), F32),
                   jax.ShapeDtypeStruct((t, k), BF16)],
        scratch_shapes=[pltpu.VMEM((LANES, k), BF16)],
        compiler_params=_params(2),
        name="ssd_dt",
    )(h, wt, norm_w.reshape(1, k), dt_bias, a_log)


def _ssd_kernel(z_ref, xs_ref, bc_ref, dt_ref, cum_ref, cwx_ref, cbx_ref, cwbc_ref, cbbc_ref,
                dskip_ref, nw_ref, o_ref, xhalo_ref, bchalo_ref, st_ref, spread_ref):
    rows = SSD_ROWS

    @pl.when(pl.program_id(0) == 0)
    def _():
        xhalo_ref[...] = jnp.zeros(xhalo_ref.shape, BF16)
        bchalo_ref[...] = jnp.zeros(bchalo_ref.shape, BF16)
        st_ref[...] = jnp.zeros(st_ref.shape, F32)
        head = lax.broadcasted_iota(jnp.int32, spread_ref.shape, 0)
        lane = lax.broadcasted_iota(jnp.int32, spread_ref.shape, 1)
        spread_ref[...] = jnp.where(lane // SSD_HEAD_DIM == head, 1.0, 0.0).astype(BF16)

    out_row = lax.broadcasted_iota(jnp.int32, (rows, CONV_HALO + rows), 0)
    in_row = lax.broadcasted_iota(jnp.int32, (rows, CONV_HALO + rows), 1)
    shifts = [jnp.where(in_row == out_row + CONV_HALO - back, 1.0, 0.0).astype(BF16)
              for back in range(SSD_CONV - 1, 0, -1)]

    def conv_silu(halo_ref, raw_ref, w_ref, b_ref):
        raw = raw_ref[...]
        ext = jnp.concatenate([halo_ref[...], raw], axis=0)
        acc = b_ref[...] + raw.astype(F32) * w_ref[SSD_CONV - 1:SSD_CONV, :]
        for k, shift in enumerate(shifts):
            acc = acc + _dot(shift, ext) * w_ref[k:k + 1, :]
        halo_ref[...] = raw[rows - CONV_HALO:, :]
        return _silu(acc)

    xs = conv_silu(xhalo_ref, xs_ref, cwx_ref, cbx_ref)
    bc = conv_silu(bchalo_ref, bc_ref, cwbc_ref, cbbc_ref)

    dt = dt_ref[...]
    cum = cum_ref[...]
    causal = _tril(rows)
    cum_t = cum.T
    low_half = lax.broadcasted_iota(jnp.int32, (rows, LANES), 1) < SSD_HEAD_DIM

    def per_lane(v, terms):
        out, rest = None, v
        for _ in range(terms):
            piece = rest.astype(BF16)
            rest = rest - piece.astype(F32)
            moved = _dot(piece, spread_ref[...])
            out = moved if out is None else out + moved
        return out

    dt_x = per_lane(dt, 2)
    cum_x = per_lane(cum, 3)

    pairs_per_group = SSD_GROUP_WIDTH // LANES
    for g in range(SSD_GROUPS):
        b_g = bc[:, g * SSD_STATE:(g + 1) * SSD_STATE]
        c_g = bc[:, (SSD_GROUPS + g) * SSD_STATE:(SSD_GROUPS + g + 1) * SSD_STATE].astype(BF16)
        cb = _dot_nt(c_g, b_g.astype(BF16))
        st = st_ref[g]
        y_off = _dot(c_g, st.astype(BF16))
        y_tiles, xw_tiles, dec_tiles = [], [], []
        for jj in range(pairs_per_group):
            j = g * pairs_per_group + jj
            h0 = 2 * j
            sl = slice(j * LANES, (j + 1) * LANES)
            x_p = xs[:, sl]
            cum_p = cum_x[:, sl]
            xdt = x_p * dt_x[:, sl]
            xdt_b = xdt.astype(BF16)
            halves = []
            for h in (h0, h0 + 1):
                diff = cum[:, h:h + 1] - cum_t[h:h + 1, :]
                decay = jnp.exp(jnp.where(causal, diff, -jnp.inf))
                halves.append(_dot((cb * decay).astype(BF16), xdt_b))
            y_diag = jnp.where(low_half, halves[0], halves[1])
            cum_last = cum_p[rows - 1:rows, :]
            xw_tiles.append((xdt * jnp.exp(cum_last - cum_p)).astype(BF16))
            dec_tiles.append(jnp.exp(cum_last))
            y = y_diag + y_off[:, jj * LANES:(jj + 1) * LANES] * jnp.exp(cum_p) + x_p * dskip_ref[:, sl]
            y_tiles.append(y * _silu(z_ref[:, sl].astype(F32)))
        xw = jnp.concatenate(xw_tiles, axis=1)
        st_ref[g] = st * jnp.concatenate(dec_tiles, axis=1) + _dot(b_g.T.astype(BF16), xw)
        ss = jnp.sum(y_tiles[0] * y_tiles[0], axis=-1, keepdims=True)
        for t in y_tiles[1:]:
            ss = ss + jnp.sum(t * t, axis=-1, keepdims=True)
        inv = lax.rsqrt(ss * (1.0 / SSD_GROUP_WIDTH) + EPS)
        for jj in range(pairs_per_group):
            sl = slice((g * pairs_per_group + jj) * LANES, (g * pairs_per_group + jj + 1) * LANES)
            o_ref[:, sl] = (y_tiles[jj] * inv * nw_ref[:, sl]).astype(o_ref.dtype)


def ssd_mixer(proj, dt, cum, conv_wx, conv_bx, conv_wbc, conv_bbc, d_skip_x, norm_w):
    t = proj.shape[0]
    rows = SSD_ROWS
    d = D_MODEL
    nbc = 2 * SSD_GROUPS * SSD_STATE
    fixed = lambda i: (0, 0)
    return pl.pallas_call(
        _ssd_kernel,
        grid=(t // rows,),
        in_specs=[pl.BlockSpec((rows, d), lambda i: (i, 0)),
                  pl.BlockSpec((rows, d), lambda i: (i, 1)),
                  pl.BlockSpec((rows, nbc), lambda i: (i, 2 * d // nbc)),
                  pl.BlockSpec((rows, LANES), lambda i: (i, 0)),
                  pl.BlockSpec((rows, LANES), lambda i: (i, 0)),
                  pl.BlockSpec((SSD_CONV, d), fixed), pl.BlockSpec((1, d), fixed),
                  pl.BlockSpec((SSD_CONV, nbc), fixed), pl.BlockSpec((1, nbc), fixed),
                  pl.BlockSpec((1, d), fixed), pl.BlockSpec((1, d), fixed)],
        out_specs=pl.BlockSpec((rows, d), lambda i: (i, 0)),
        out_shape=jax.ShapeDtypeStruct((t, d), BF16),
        scratch_shapes=[pltpu.VMEM((CONV_HALO, d), BF16),
                        pltpu.VMEM((CONV_HALO, nbc), BF16),
                        pltpu.VMEM((SSD_GROUPS, SSD_STATE, SSD_GROUP_WIDTH), F32),
                        pltpu.VMEM((LANES, d), BF16)],
        compiler_params=_params(1),
        name="ssd_mixer",
    )(proj, proj, proj, dt, cum, conv_wx, conv_bx, conv_wbc, conv_bbc, d_skip_x, norm_w)


def _hgrn_kernel(q_ref, v_ref, g_ref, f_ref, lb_ref, nw_ref, o_ref, st_ref):
    rows = HGRN_ROWS

    @pl.when(pl.program_id(0) == 0)
    def _():
        st_ref[...] = jnp.zeros(st_ref.shape, F32)

    lb = lb_ref[...]
    causal = _tril(rows)
    tri = causal.astype(F32)
    for c in range(HGRN_CHUNKS_PER_STEP):
        r = slice(c * rows, (c + 1) * rows)
        f = lb + (1.0 - lb) * jax.nn.sigmoid(f_ref[r, :])
        k = 1.0 - f
        cum = _dot_f32(tri, jnp.log(f))
        qf = _silu(q_ref[r, :].astype(F32))
        mid = cum[rows // 2 - 1:rows // 2, :]
        last = cum[rows - 1:rows, :]
        q_rel = (qf * jnp.exp(cum - mid)).astype(BF16)
        k_rel = (k * jnp.exp(mid - cum)).astype(BF16)
        k_end = (k * jnp.exp(last - cum)).astype(BF16)
        q_dec = (qf * jnp.exp(cum)).astype(BF16)
        chunk_decay = jnp.exp(last)
        for h in range(HGRN_HEADS):
            sl = slice(h * HGRN_DIM, (h + 1) * HGRN_DIM)
            v_h = v_ref[r, sl]
            att = jnp.where(causal, _dot_nt(q_rel[:, sl], k_rel[:, sl]), 0.0)
            st = st_ref[h]
            o = _dot(att.astype(BF16), v_h) + _dot_nt(q_dec[:, sl], st.astype(BF16))
            st_ref[h] = st * chunk_decay[:, sl] + _dot_tn(v_h, k_end[:, sl])
            on = o * lax.rsqrt(jnp.mean(o * o, axis=-1, keepdims=True) + EPS) * nw_ref[:, sl]
            o_ref[r, sl] = (on * _silu(g_ref[r, sl].astype(F32))).astype(o_ref.dtype)


def hgrn_mixer(qvg, f_raw, lb, norm_w):
    t = qvg.shape[0]
    rows = HGRN_ROWS * HGRN_CHUNKS_PER_STEP
    d = D_MODEL
    fixed = lambda i: (0, 0)
    return pl.pallas_call(
        _hgrn_kernel,
        grid=(t // rows,),
        in_specs=[pl.BlockSpec((rows, d), lambda i: (i, 0)),
                  pl.BlockSpec((rows, d), lambda i: (i, 1)),
                  pl.BlockSpec((rows, d), lambda i: (i, 2)),
                  pl.BlockSpec((rows, d), lambda i: (i, 0)),
                  pl.BlockSpec((1, d), fixed), pl.BlockSpec((1, d), fixed)],
        out_specs=pl.BlockSpec((rows, d), lambda i: (i, 0)),
        out_shape=jax.ShapeDtypeStruct((t, d), BF16),
        scratch_shapes=[pltpu.VMEM((HGRN_HEADS, HGRN_DIM, HGRN_DIM), F32)],
        compiler_params=_params(1),
        name="hgrn_mixer",
    )(qvg, qvg, qvg, f_raw, lb, norm_w)


def _forget_gate_kernel(x_ref, w_ref, nw_ref, b_ref, c_ref, ct_ref, hn_ref, wb_ref, carry_ref):
    rows = x_ref.shape[0]

    @pl.when(pl.program_id(1) == 0)
    def _():
        carry_ref[...] = jnp.zeros(carry_ref.shape, F32)

    _cast_weights_once([(w_ref, wb_ref)])
    f_raw = _dot_nt(_normed_rows(x_ref, nw_ref, hn_ref), wb_ref[...])
    log_f = -_softplus(-(f_raw + b_ref[...]))
    tri = _tril(CUMSUM_ROWS).astype(F32)
    carry = carry_ref[...]
    for s in range(rows // CUMSUM_ROWS):
        r = slice(s * CUMSUM_ROWS, (s + 1) * CUMSUM_ROWS)
        c = _dot_f32(tri, log_f[r]) + carry
        carry = c[CUMSUM_ROWS - 1:CUMSUM_ROWS, :]
        c2 = c * LOG2E
        c_ref[r, :] = c2
        ct_ref[:, r] = c2.T
    carry_ref[...] = carry


def forget_gate_cumsum(h, norm_w, wt, layer, row0, b_f, tm=1024):
    t, k = h.shape
    return pl.pallas_call(
        _forget_gate_kernel,
        grid=(1, t // tm),
        in_specs=[pl.BlockSpec((tm, k), lambda j, i: (i, 0)),
                  _weight_window(layer, LANES, k, lambda j: row0 // SUBLANES, lambda j: 0),
                  pl.BlockSpec((1, k), lambda j, i: (0, 0)), pl.BlockSpec((1, LANES), lambda j, i: (0, 0))],
        out_specs=[pl.BlockSpec((tm, LANES), lambda j, i: (i, 0)), pl.BlockSpec((LANES, tm), lambda j, i: (0, i)),
                   pl.BlockSpec((tm, k), lambda j, i: (i, 0))],
        out_shape=[jax.ShapeDtypeStruct((t, LANES), F32), jax.ShapeDtypeStruct((LANES, t), F32),
                   jax.ShapeDtypeStruct((t, k), BF16)],
        scratch_shapes=[pltpu.VMEM((LANES, k), BF16), pltpu.VMEM((1, LANES), F32)],
        compiler_params=_params(2),
        name="forget_gate_cumsum",
    )(h, wt, norm_w.reshape(1, k), b_f)


def _fox_kernel(q_ref, k_ref, v_ref, dkc_ref, dqr_ref, o_ref):
    tq, tk = FOX_Q_BLOCK, FOX_K_BLOCK
    half = tq // 2
    hp = pl.program_id(0)
    qi = pl.program_id(1)
    q_start = pl.multiple_of(qi * tq, tq)
    heads = []
    for hh in range(FOX_PACK):
        sl = slice(hh * FOX_DIM, (hh + 1) * FOX_DIM)
        heads.append((hh, sl, q_ref[:, sl], dqr_ref[hh, :, pl.ds(q_start, tq)]))

    def step(start, width, lo, carry, diagonal):
        out = []
        lane = lax.broadcasted_iota(jnp.int32, (width, LANES), 1)
        dk_all = dkc_ref[pl.ds(start, width), :]
        for (hh, sl, q, dq), (m, l, acc) in zip(heads, carry):
            k_b = k_ref[pl.ds(start, width), sl]
            v_b = v_ref[pl.ds(start, width), sl]
            dk = jnp.sum(jnp.where(lane == FOX_F_LANE0 + hp * FOX_PACK + hh, dk_all, 0.0), axis=-1, keepdims=True)
            s = _dot_nt(k_b, q[lo:]) - dk
            if diagonal:
                key = lax.broadcasted_iota(jnp.int32, s.shape, 0)
                qry = lax.broadcasted_iota(jnp.int32, s.shape, 1)
                s = jnp.where(qry >= key, s, -jnp.inf)
            m_new = jnp.maximum(m, jnp.max(s, axis=0, keepdims=True) + dq[:, lo:])
            alpha = jnp.exp2(m - m_new)
            p = jnp.exp2(s - (m_new - dq[:, lo:]))
            l = alpha * l + jnp.sum(p, axis=0, keepdims=True)
            acc = alpha * acc + _dot_tn(v_b, p.astype(BF16))
            out.append((m_new, l, acc))
        return tuple(out)

    init = tuple((jnp.full((1, tq), -jnp.inf, F32), jnp.zeros((1, tq), F32), jnp.zeros((FOX_DIM, tq), F32))
                 for _ in heads)
    n_wide = lax.shift_right_logical(qi, 1)
    carry = lax.fori_loop(0, n_wide, lambda ki, c: step(pl.multiple_of(ki * tk, tk), tk, 0, c, False), init)
    odd_start = pl.multiple_of(n_wide * tk, tk)
    carry = lax.fori_loop(0, qi & 1, lambda _, c: step(odd_start, tq, 0, c, False), carry)
    carry = step(q_start, half, 0, carry, True)
    hi = step(pl.multiple_of(q_start + half, half), half, half,
              tuple((m[:, half:], l[:, half:], acc[:, half:]) for m, l, acc in carry), True)
    for (hh, sl, _, _), (m, l, acc), (m_hi, l_hi, acc_hi) in zip(heads, carry, hi):
        l = jnp.concatenate([l[:, :half], l_hi], axis=1)
        acc = jnp.concatenate([acc[:, :half], acc_hi], axis=1)
        o_ref[:, sl] = (acc / l).T.astype(o_ref.dtype)


def fox_attention(qkv, dcum, dcum_t):
    t = qkv.shape[0]
    tq = FOX_Q_BLOCK
    width = FOX_PACK * FOX_DIM
    groups = FOX_HEADS // FOX_PACK
    return pl.pallas_call(
        _fox_kernel,
        grid=(groups, t // tq),
        in_specs=[pl.BlockSpec((tq, width), lambda h, i: (i, h)),
                  pl.BlockSpec((t, width), lambda h, i: (0, groups + h)),
                  pl.BlockSpec((t, width), lambda h, i: (0, 2 * groups + h)),
                  pl.BlockSpec((t, LANES), lambda h, i: (0, 0)),
                  pl.BlockSpec((FOX_PACK, 1, t), lambda h, i: (h, 0, 0))],
        out_specs=pl.BlockSpec((tq, width), lambda h, i: (i, h)),
        out_shape=jax.ShapeDtypeStruct((t, FOX_HEADS * FOX_DIM), BF16),
        compiler_params=_params(2),
        name="fox_attention",
    )(qkv, qkv, qkv, dcum, dcum_t)


def _swiglu_half(h, hn, norm_w, w_in, w_out, layer):
    n_main = (D_FF // FFN_TILE) * FFN_TILE
    n_tail = D_FF - n_main
    if hn is None:
        act_tail, hn, w_tail = ffn_in(h, w_in, layer, n_main, n_tail, n_tail, norm_w=norm_w, w_out=w_out)
    else:
        act_tail, w_tail = ffn_in(hn, w_in, layer, n_main, n_tail, n_tail, w_out=w_out)
    act_main, w_main = ffn_in(hn, w_in, layer, 0, n_main, FFN_TILE, w_out=w_out)
    return matmul_residual(h, None, 0, [(act_main, w_main), (act_tail, w_tail)], scale=0.5, tn=1024)


def _ssd_hgrn_layer(h, norm_w, w_in, layer, conv_w, conv_b, dt_bias, a_log, d_skip, ssd_norm_w, lb, hgrn_norm_w, w_out):
    d = D_MODEL
    nb = SSD_GROUPS * SSD_STATE
    o_dt = 2 * d + 2 * nb
    o_q = o_dt + SSD_HEADS
    tile = 1024 // SUBLANES
    wt = jnp.swapaxes(w_in, 1, 2)
    pad_heads = lambda v: jnp.pad(v, (0, LANES - SSD_HEADS)).reshape(1, LANES)
    dt, cum, hn = ssd_dt(h, norm_w, wt, layer, o_dt, pad_heads(dt_bias), pad_heads(a_log))
    proj = matmul_nt(hn, wt, layer, BF16, o_dt, lambda j: j * tile)
    first = o_q // SUBLANES
    qvg = matmul_nt(hn, wt, layer, BF16, 3 * d, lambda j: first + (j + jnp.where(j >= 2, 2, 0)) * tile)
    f_raw = matmul_nt(hn, wt, layer, F32, d, lambda j: first + (j + 2) * tile)
    y_a = ssd_mixer(proj, dt, cum, conv_w[:, :d], conv_b[:d].reshape(1, d), conv_w[:, d:],
                    conv_b[d:].reshape(1, 2 * nb),
                    jnp.repeat(d_skip, SSD_HEAD_DIM).reshape(1, d), ssd_norm_w.reshape(1, d))
    y_b = hgrn_mixer(qvg, f_raw, lb.reshape(1, d), hgrn_norm_w.reshape(1, d))
    return matmul_residual(h, w_out, layer, [(y_a, 0), (y_b, d)], tn=1024, weight_buffers=1)


def _fox_layer(h, norm_w, w_in, layer, b_f, w_out):
    d = D_MODEL
    t = h.shape[0]
    tile = 1024 // SUBLANES
    wt = jnp.swapaxes(w_in, 1, 2)
    b_pad = jnp.pad(b_f, (FOX_F_LANE0, 0)).reshape(1, LANES)
    dcum, dcum_t, hn = forget_gate_cumsum(h, norm_w, wt, layer, 3 * d + FOX_HEADS - LANES, b_pad)
    qkv = matmul_nt(hn, wt, layer, BF16, 3 * d, lambda j: j * tile, scaled_blocks=d // 1024,
                    scale=LOG2E * FOX_DIM ** -0.5)
    o = fox_attention(qkv, dcum, dcum_t[FOX_F_LANE0:].reshape(FOX_HEADS, 1, t))
    return matmul_residual(h, w_out, layer, [(o, 0)], tm=1024, tn=1024)


def kernel(x, p, ffn1_norm, ffn1_w_in, ffn1_w_out, mix_norm, ab_w_in, ssd_conv_w, ssd_conv_b, ssd_dt_bias,
           ssd_a_log, ssd_d, ssd_norm, hgrn_lb_logits, hgrn_norm, ab_w_out, fox_w_in, fox_b_f, fox_w_out,
           ffn2_norm, ffn2_w_in, ffn2_w_out, ple_gate_norm, ple_w_gate, ple_w_up, ple_norm, final_norm):
    bsz, t, d = x.shape
    depth = p.shape[0]
    assert bsz == 1 and d == D_MODEL
    lb_all = jnp.cumsum(jax.nn.softmax(hgrn_lb_logits.astype(F32), axis=0), axis=0)
    h = x.reshape(t, d)
    hn = None
    for i in range(depth):
        j = i // 2
        h = _swiglu_half(h, hn, ffn1_norm[i], ffn1_w_in, ffn1_w_out, i)
        if i % 2 == 0:
            h = _ssd_hgrn_layer(h, mix_norm[i], ab_w_in, j, ssd_conv_w[j], ssd_conv_b[j], ssd_dt_bias[j],
                                ssd_a_log[j], ssd_d[j], ssd_norm[j], lb_all[i], hgrn_norm[j], ab_w_out)
        else:
            h = _fox_layer(h, mix_norm[i], fox_w_in, j, fox_b_f[j], fox_w_out)
        h = _swiglu_half(h, None, ffn2_norm[i], ffn2_w_in, ffn2_w_out, i)
        final = i == depth - 1
        next_w = final_norm if final else ffn1_norm[i + 1]
        outs = ple_add(h, p, ple_w_gate, ple_w_up, i, ple_gate_norm[i], ple_norm[i], next_w, final)
        if final:
            return outs[0].reshape(bsz, t, d)
        h, hn = outs
```
